```python
import math
import jax
import jax.numpy as jnp
from jax import lax
import numpy as np

D_MODEL = 1024
BATCH = 4
SEQ = 4096
DEPTH = 2

HEAD = 64
D_MIX = D_MODEL
NORM_EPS = 1e-6

RW_WIDTH = D_MIX // 4
RW_HEADS = RW_WIDTH // HEAD
RW_W_RANK = 64
RW_A_RANK = 64
RW_G_RANK = 128
RW_DECAY_SCALE = math.exp(-0.5)
RW_LN_EPS = 64e-5
RW_SIZES = (RW_WIDTH, RW_WIDTH, RW_WIDTH, RW_W_RANK, RW_A_RANK, RW_G_RANK)
RW_PROJ = sum(RW_SIZES)

LRU_WIDTH = D_MIX // 2
LRU_BLOCKS = LRU_WIDTH // HEAD
LRU_CONV = 4
LRU_C = 8.0
LRU_SIZES = (LRU_WIDTH, LRU_WIDTH)
LRU_PROJ = sum(LRU_SIZES)

GLA_WIDTH = D_MIX - RW_WIDTH - LRU_WIDTH
GLA_HEADS = 4
GLA_DV = GLA_WIDTH // GLA_HEADS
GLA_DK = GLA_DV // 2
GLA_KEY_WIDTH = GLA_HEADS * GLA_DK
GLA_GATE_RANK = 16
GLA_GATE_NORM = 16.0
GLA_CHUNK = 64
GLA_SIZES = (GLA_KEY_WIDTH, GLA_KEY_WIDTH, GLA_WIDTH, GLA_GATE_RANK, GLA_WIDTH)
GLA_PROJ = sum(GLA_SIZES)

P_IN = RW_PROJ + LRU_PROJ + GLA_PROJ
D_FF = ((8 * D_MODEL + 3 * 256 - 1) // (3 * 256)) * 256

kernel_name = "hybrid_rwkv7_rglru_gla_block"


def _split(z, sizes):
    return jnp.split(z, np.cumsum(sizes)[:-1].tolist(), axis=-1)


def rms_norm(x, g, eps=NORM_EPS):
    xf = x.astype(jnp.float32)
    y = xf * lax.rsqrt(jnp.mean(xf * xf, axis=-1, keepdims=True) + eps)
    return (y * g.astype(jnp.float32)).astype(x.dtype)


def token_shift(z):
    return jnp.pad(z, ((0, 0), (1, 0), (0, 0)))[:, :-1]


def wkv7_scan(r, w, k, v, a, b):
    bsz, _, nh, n = r.shape

    def step(S, inp):
        r_t, w_t, k_t, v_t, a_t, b_t = inp
        sa = jnp.einsum('bhvk,bhk->bhv', S, a_t)
        S = (S * w_t[:, :, None, :] + sa[..., None] * b_t[:, :, None, :]
             + v_t[..., None] * k_t[:, :, None, :])
        return S, jnp.einsum('bhvk,bhk->bhv', S, r_t)

    xs = tuple(jnp.swapaxes(z, 0, 1) for z in (r, w, k, v, a, b))
    s0 = jnp.zeros((bsz, nh, n, n), jnp.float32)
    _, ys = lax.scan(step, s0, xs)
    return jnp.swapaxes(ys, 0, 1)


def rwkv7_mixer(p, mu, w0, w_up, a0, a_up, g_up, k_k, k_a, r_k, ln_g, ln_b):
    bsz, t, _ = p.shape
    f32 = jnp.float32
    p = p + (token_shift(p) - p) * mu
    r, k, v, w_lo, a_lo, g_lo = _split(p, RW_SIZES)
    w_raw = w0 + jnp.tanh(w_lo) @ w_up
    decay = jnp.exp(-RW_DECAY_SCALE * jax.nn.sigmoid(w_raw.astype(f32)))
    a = jax.nn.sigmoid(a0 + a_lo @ a_up)
    g = jax.nn.sigmoid(g_lo) @ g_up

    def hs(z):
        return z.reshape(bsz, t, RW_HEADS, HEAD).astype(f32)

    kk = hs(k * k_k)
    kk = kk / jnp.maximum(jnp.sqrt(jnp.sum(kk * kk, axis=-1, keepdims=True)), 1e-12)
    k = k * (1 + (a - 1) * k_a)
    r_h, k_h, v_h, a_h = hs(r), hs(k), hs(v), hs(a)
    y = wkv7_scan(r_h, hs(decay), k_h, v_h, -kk, kk * a_h)
    mean = jnp.mean(y, axis=-1, keepdims=True)
    var = jnp.mean(jnp.square(y - mean), axis=-1, keepdims=True)
    y = (y - mean) * lax.rsqrt(var + RW_LN_EPS)
    y = y * ln_g.reshape(RW_HEADS, HEAD).astype(f32) + ln_b.reshape(RW_HEADS, HEAD).astype(f32)
    y = y + jnp.sum(r_h * k_h * r_k.astype(f32), axis=-1, keepdims=True) * v_h
    return (y.reshape(bsz, t, RW_WIDTH) * g.astype(f32)).astype(p.dtype)


def causal_depthwise_conv(z, w, b):
    t = z.shape[1]
    zp = jnp.pad(z, ((0, 0), (LRU_CONV - 1, 0), (0, 0)))
    return b + sum(zp[:, j:j + t] * w[j] for j in range(LRU_CONV))


def rglru_mixer(p, conv_w, conv_b, wa, ba, wx, bx, lam, norm_g):
    bsz, t, _ = p.shape
    f32 = jnp.float32
    xb, gate = _split(p, LRU_SIZES)
    xc = causal_depthwise_conv(xb, conv_w, conv_b)
    xblk = xc.reshape(bsz, t, LRU_BLOCKS, HEAD)
    gate_r = jax.nn.sigmoid(jnp.einsum('btnc,ncd->btnd', xblk, wa).reshape(bsz, t, LRU_WIDTH) + ba)
    gate_i = jax.nn.sigmoid(jnp.einsum('btnc,ncd->btnd', xblk, wx).reshape(bsz, t, LRU_WIDTH) + bx)
    log_a = (-LRU_C * gate_r.astype(f32)) * jax.nn.softplus(-lam.astype(f32))
    a = jnp.exp(log_a)
    u = jnp.sqrt(-jnp.expm1(2.0 * log_a)) * (gate_i * xc).astype(f32)

    def combine(left, right):
        a_l, u_l = left
        a_r, u_r = right
        return a_l * a_r, a_r * u_l + u_r

    _, h = lax.associative_scan(combine, (a, u), axis=1)
    y = h * jax.nn.gelu(gate.astype(f32))
    y = rms_norm(y.reshape(bsz, t, LRU_BLOCKS, HEAD), norm_g.reshape(LRU_BLOCKS, HEAD))
    return y.reshape(bsz, t, LRU_WIDTH).astype(p.dtype)


def gla_chunked(q, k, v, log_a):
    bsz, nh, t, dk = q.shape
    dv = v.shape[-1]
    c = GLA_CHUNK
    n = t // c
    q, k, v, log_a = (z.reshape(bsz, nh, n, c, z.shape[-1]) for z in (q, k, v, log_a))
    bcum = jnp.cumsum(log_a, axis=3)
    q_in = q * jnp.exp(bcum)
    k_in = k * jnp.exp(-bcum)
    causal = jnp.tril(jnp.ones((c, c), dtype=bool))
    scores = jnp.where(causal, jnp.einsum('bhnid,bhnjd->bhnij', q_in, k_in), 0.0)
    o_intra = jnp.einsum('bhnij,bhnjv->bhniv', scores, v)
    b_last = bcum[:, :, :, -1:, :]
    chunk_kv = jnp.einsum('bhncd,bhncv->bhndv', k * jnp.exp(b_last - bcum), v)
    chunk_decay = jnp.exp(b_last[:, :, :, 0, :])

    def step(S, inp):
        dec, kv = inp
        return S * dec[..., None] + kv, S

    s0 = jnp.zeros((bsz, nh, dk, dv), jnp.float32)
    _, s_prev = lax.scan(step, s0, (jnp.moveaxis(chunk_decay, 2, 0), jnp.moveaxis(chunk_kv, 2, 0)))
    s_prev = jnp.moveaxis(s_prev, 0, 2)
    o_inter = jnp.einsum('bhncd,bhndv->bhncv', q_in, s_prev)
    return (o_intra + o_inter).reshape(bsz, nh, t, dv)


def gla_mixer(p, gk_up, gk_b, norm_g):
    bsz, t, _ = p.shape
    f32 = jnp.float32
    q, k, v, gk_lo, g = _split(p, GLA_SIZES)
    log_a = jax.nn.log_sigmoid((gk_lo @ gk_up + gk_b).astype(f32)) / GLA_GATE_NORM

    def heads(z, d):
        return jnp.swapaxes(z.reshape(bsz, t, GLA_HEADS, d), 1, 2).astype(f32)

    o = gla_chunked(heads(q, GLA_DK) * GLA_DK ** -0.5, heads(k, GLA_DK),
                    heads(v, GLA_DV), heads(log_a, GLA_DK))
    o = jnp.swapaxes(o, 1, 2)
    o = rms_norm(o, norm_g) * jax.nn.silu(g.reshape(bsz, t, GLA_HEADS, GLA_DV).astype(f32))
    return o.reshape(bsz, t, GLA_WIDTH).astype(p.dtype)


def setup_inputs(seed: int = 0) -> dict:
    key = jax.random.key(seed)
    ks = iter(jax.random.split(key, 40))
    L = DEPTH

    def nrm(shape, scale):
        return scale * jax.random.normal(next(ks), shape, jnp.float32)

    def uni(shape, lo, hi):
        return jax.random.uniform(next(ks), shape, jnp.float32, lo, hi)

    a_init = uni((L, LRU_WIDTH), 0.9, 0.999)
    s = a_init ** (1.0 / LRU_C)
    lam = jnp.log(s) - jnp.log1p(-s)
    return {
        "x": nrm((BATCH, SEQ, D_MODEL), 1.0),
        "norm1_g": 1.0 + nrm((L, D_MODEL), 0.02),
        "w_in": nrm((L, D_MODEL, P_IN), D_MODEL ** -0.5),
        "rw_mu": uni((L, RW_PROJ), 0.0, 1.0),
        "rw_w0": uni((L, RW_WIDTH), -4.0, 1.0),
        "rw_w_up": nrm((L, RW_W_RANK, RW_WIDTH), 0.5 * RW_W_RANK ** -0.5),
        "rw_a0": nrm((L, RW_WIDTH), 0.1),
        "rw_a_up": nrm((L, RW_A_RANK, RW_WIDTH), 0.5 * RW_A_RANK ** -0.5),
        "rw_g_up": nrm((L, RW_G_RANK, RW_WIDTH), RW_G_RANK ** -0.5),
        "rw_k_k": 0.85 + nrm((L, RW_WIDTH), 0.02),
        "rw_k_a": 1.0 + nrm((L, RW_WIDTH), 0.02),
        "rw_r_k": nrm((L, RW_HEADS, HEAD), 0.1),
        "rw_ln_g": 1.0 + nrm((L, RW_WIDTH), 0.02),
        "rw_ln_b": nrm((L, RW_WIDTH), 0.01),
        "lru_conv_w": nrm((L, LRU_CONV, LRU_WIDTH), LRU_CONV ** -0.5),
        "lru_conv_b": nrm((L, LRU_WIDTH), 0.01),
        "lru_wa": nrm((L, LRU_BLOCKS, HEAD, HEAD), HEAD ** -0.5),
        "lru_ba": nrm((L, LRU_WIDTH), 0.01),
        "lru_wx": nrm((L, LRU_BLOCKS, HEAD, HEAD), HEAD ** -0.5),
        "lru_bx": nrm((L, LRU_WIDTH), 0.01),
        "lru_lam": lam,
        "lru_norm_g": 1.0 + nrm((L, LRU_WIDTH), 0.02),
        "gla_gk_up": nrm((L, GLA_GATE_RANK, GLA_KEY_WIDTH), GLA_GATE_RANK ** -0.5),
        "gla_gk_b": nrm((L, GLA_KEY_WIDTH), 0.1),
        "gla_norm_g": 1.0 + nrm((L, GLA_DV), 0.02),
        "w_out": nrm((L, D_MIX, D_MODEL), D_MIX ** -0.5),
        "norm2_g": 1.0 + nrm((L, D_MODEL), 0.02),
        "ffn_w_gate": nrm((L, D_MODEL, D_FF), D_MODEL ** -0.5),
        "ffn_w_up": nrm((L, D_MODEL, D_FF), D_MODEL ** -0.5),
        "ffn_w_down": nrm((L, D_FF, D_MODEL), D_FF ** -0.5),
        "final_norm_g": 1.0 + nrm((D_MODEL,), 0.02),
    }


def reference(x, norm1_g, w_in, rw_mu, rw_w0, rw_w_up, rw_a0, rw_a_up, rw_g_up, rw_k_k,
              rw_k_a, rw_r_k, rw_ln_g, rw_ln_b, lru_conv_w, lru_conv_b, lru_wa, lru_ba,
              lru_wx, lru_bx, lru_lam, lru_norm_g, gla_gk_up, gla_gk_b, gla_norm_g, w_out,
              norm2_g, ffn_w_gate, ffn_w_up, ffn_w_down, final_norm_g):
    for l in range(DEPTH):
        hn = rms_norm(x, norm1_g[l])
        p_rw, p_lru, p_gla = _split(hn @ w_in[l], (RW_PROJ, LRU_PROJ, GLA_PROJ))
        y_rw = rwkv7_mixer(p_rw, rw_mu[l], rw_w0[l], rw_w_up[l], rw_a0[l], rw_a_up[l],
                           rw_g_up[l], rw_k_k[l], rw_k_a[l], rw_r_k[l], rw_ln_g[l], rw_ln_b[l])
        y_lru = rglru_mixer(p_lru, lru_conv_w[l], lru_conv_b[l], lru_wa[l], lru_ba[l],
                            lru_wx[l], lru_bx[l], lru_lam[l], lru_norm_g[l])
        y_gla = gla_mixer(p_gla, gla_gk_up[l], gla_gk_b[l], gla_norm_g[l])
        x = x + jnp.concatenate([y_rw, y_lru, y_gla], axis=-1) @ w_out[l]
        hn = rms_norm(x, norm2_g[l])
        x = x + (jax.nn.silu(hn @ ffn_w_gate[l]) * (hn @ ffn_w_up[l])) @ ffn_w_down[l]
    return rms_norm(x, final_norm_g)
```

```python
import functools
import math

import jax
import jax.numpy as jnp
from jax import lax
from jax.experimental import pallas as pl
from jax.experimental.pallas import tpu as pltpu

F32 = jnp.float32
BF16 = jnp.bfloat16

HEAD = 64
NORM_EPS = 1e-6
RW_WIDTH = 256
RW_DECAY_SCALE = math.exp(-0.5)
RW_LN_EPS = 64e-5
LRU_WIDTH = 512
LRU_CONV = 4
LRU_C = 8.0
GLA_WIDTH = 256
GLA_KEY_WIDTH = 128
GLA_DK = 32
GLA_GATE_RANK = 16
GLA_GATE_NORM = 16.0
GLA_PAD = 896

CHUNK = 64
LANES = 128
SUBLANES = 8
VMEM_LIMIT = 56 * 1024 * 1024


def _dot(a, b):
    return jnp.dot(a.astype(BF16), b.astype(BF16), preferred_element_type=F32)


def _dot_nt(a, b):
    return lax.dot_general(a.astype(BF16), b.astype(BF16), (((1,), (1,)), ((), ())),
                           preferred_element_type=F32)


def _dot_tn(a, b):
    return lax.dot_general(a.astype(BF16), b.astype(BF16), (((0,), (0,)), ((), ())),
                           preferred_element_type=F32)


def _split(x):
    hi = x.astype(BF16)
    lo = (x - hi.astype(F32)).astype(BF16)
    return hi, lo


def _dot_exact_rhs(x, w):
    hi, lo = _split(x)
    return (jnp.dot(hi, w, preferred_element_type=F32)
            + jnp.dot(lo, w, preferred_element_type=F32))


def _dot_exact_lhs(w, x):
    hi, lo = _split(x)
    return (jnp.dot(w, hi, preferred_element_type=F32)
            + jnp.dot(w, lo, preferred_element_type=F32))


def _dot3(a, b):
    ah, al = _split(a)
    bh, bl = _split(b)
    return (jnp.dot(ah, bh, preferred_element_type=F32)
            + jnp.dot(al, bh, preferred_element_type=F32)
            + jnp.dot(ah, bl, preferred_element_type=F32))


def _iota(shape, dim):
    return lax.broadcasted_iota(jnp.int32, shape, dim)


def _group_ones(n, group):
    r = _iota((n, n), 0) // group
    c = _iota((n, n), 1) // group
    return jnp.where(r == c, 1.0, 0.0).astype(BF16)


def _tril_ones(n):
    return jnp.where(_iota((n, n), 0) >= _iota((n, n), 1), 1.0, 0.0).astype(BF16)


def _rms_rows(x, g):
    ms = jnp.mean(x * x, axis=-1, keepdims=True)
    return x * lax.rsqrt(ms + NORM_EPS) * g


def _shift_rows(x, prev_tail, j):
    xs = pltpu.roll(x, j, 0)
    fix = pltpu.roll(prev_tail, j, 0)
    row = _iota((SUBLANES, x.shape[1]), 0)
    head = jnp.where(row < j, fix, xs[0:SUBLANES])
    return jnp.concatenate([head, xs[SUBLANES:]], axis=0)


def _sigmoid(x):
    return 1.0 / (1.0 + jnp.exp(-x))


def _inproj_kernel(x_ref, g_ref, wrw_ref, wlru_ref, wgla_ref, prw_ref, plru_ref, pgla_ref):
    hn = _rms_rows(x_ref[...], g_ref[...]).astype(BF16)
    prw_ref[...] = jnp.dot(hn, wrw_ref[...], preferred_element_type=F32)
    plru_ref[...] = jnp.dot(hn, wlru_ref[...], preferred_element_type=F32)
    pgla_ref[...] = jnp.dot(hn, wgla_ref[...], preferred_element_type=F32)


def _const_spec(shape):
    nd = len(shape)
    return pl.BlockSpec(shape, lambda *_: (0,) * nd)


def _inproj(xf, g, w_rw, w_lru, w_gla, tm):
    n, d = xf.shape
    row = lambda c: pl.BlockSpec((tm, c), lambda i: (i, 0))
    return pl.pallas_call(
        _inproj_kernel,
        grid=(n // tm,),
        in_specs=[row(d), _const_spec(g.shape), _const_spec(w_rw.shape),
                  _const_spec(w_lru.shape), _const_spec(w_gla.shape)],
        out_specs=[row(w_rw.shape[1]), row(w_lru.shape[1]), row(w_gla.shape[1])],
        out_shape=[jax.ShapeDtypeStruct((n, w.shape[1]), F32) for w in (w_rw, w_lru, w_gla)],
        compiler_params=pltpu.CompilerParams(dimension_semantics=("arbitrary",),
                                             vmem_limit_bytes=VMEM_LIMIT),
        name="inproj",
    )(xf, g, w_rw, w_lru, w_gla)


def _inv_unit_lower(n_mat, eye, rr, cc):
    nd = jnp.where((rr // 4) == (cc // 4), n_mat, 0.0)
    x = eye + nd
    t = x + _dot(_dot(nd, nd), x)
    s = 4
    while s < CHUNK:
        off = ((rr // (2 * s)) == (cc // (2 * s))) & ((rr // s) != (cc // s))
        n_off = jnp.where(off, n_mat, 0.0)
        t = t + _dot(t, _dot(n_off, t))
        s *= 2
    return t


def _rwkv_kernel(p_ref, mu_ref, w0_ref, wup_ref, a0_ref, aup_ref, gup_ref, kk_ref, ka_ref,
                 rk_ref, lng_ref, lnb_ref, y_ref,
                 tail_ref, s_ref, r_s, k_s, v_s, a_s, b_s, lw_s, o_s, *, tb):
    @pl.when(pl.program_id(1) == 0)
    def _():
        tail_ref[...] = jnp.zeros_like(tail_ref)
        s_ref[...] = jnp.zeros_like(s_ref)

    p = p_ref[...]
    shifted = _shift_rows(p, tail_ref[...], 1)
    tail_ref[...] = p[tb - SUBLANES:tb, :]
    ps = p + (shifted - p) * mu_ref[...]
    r = ps[:, 0:256]
    k = ps[:, 256:512]
    v = ps[:, 512:768]
    lo_wa = ps[:, 768:896]
    g_lo = ps[:, 896:1024]
    w_raw = w0_ref[...] + _dot(jnp.tanh(lo_wa), wup_ref[...])
    lw = -RW_DECAY_SCALE * _sigmoid(w_raw)
    a = _sigmoid(a0_ref[...] + _dot(lo_wa, aup_ref[...]))
    g = _dot(_sigmoid(g_lo), gup_ref[...])
    ones_h = _group_ones(RW_WIDTH, HEAD)
    kk = k * kk_ref[...]
    kk = kk / jnp.maximum(jnp.sqrt(_dot_exact_rhs(kk * kk, ones_h)), 1e-12)
    k = k * (1.0 + (a - 1.0) * ka_ref[...])
    r_s[...] = r
    k_s[...] = k
    v_s[...] = v
    a_s[...] = -kk
    b_s[...] = kk * a
    lw_s[...] = lw

    tril = _tril_ones(CHUNK)
    rr = _iota((LANES, LANES), 0)
    cc = _iota((LANES, LANES), 1)
    same = (rr // CHUNK) == (cc // CHUNK)
    strict = same & (cc < rr)
    incl = same & (cc <= rr)
    eye = jnp.where(rr == cc, 1.0, 0.0).astype(F32)
    lane = _iota((CHUNK, LANES), 1)
    m0 = lane < HEAD

    def stack(x):
        return jnp.concatenate([jnp.where(m0, x, 0.0), jnp.where(m0, 0.0, x)], axis=0)

    def tile(x):
        return jnp.concatenate([x, x], axis=0)

    def chunk_body(c, carry):
        rows = pl.ds(pl.multiple_of(c * CHUNK, CHUNK), CHUNK)
        rc, kc, vc, ac, bc, lwc = (z[rows, :] for z in (r_s, k_s, v_s, a_s, b_s, lw_s))
        cs = _dot_exact_lhs(tril, lwc)
        cl = cs[CHUNK - 1:CHUNK, :]
        w_in = jnp.exp(cs)
        w_ex = jnp.exp(cs - lwc)
        inv = jnp.exp(-cs)
        dec = jnp.exp(cl - cs)
        wl = jnp.exp(cl)
        at, rt, bt, kt = ac * w_ex, rc * w_in, bc * inv, kc * inv
        bd, kd = bc * dec, kc * dec
        for pair in range(2):
            sl = slice(pair * LANES, (pair + 1) * LANES)
            ar = jnp.concatenate([stack(at[:, sl]), stack(rt[:, sl])], axis=0)
            bk = jnp.concatenate([tile(bt[:, sl]), tile(kt[:, sl])], axis=0)
            gm = _dot_nt(ar, bk)
            n_mat = jnp.where(strict, gm[0:LANES, 0:LANES], 0.0)
            l_ak = jnp.where(strict, gm[0:LANES, LANES:], 0.0)
            l_rb = jnp.where(incl, gm[LANES:, 0:LANES], 0.0)
            l_rk = jnp.where(incl, gm[LANES:, LANES:], 0.0)
            t_inv = _inv_unit_lower(n_mat, eye, rr, cc)
            v_bd = stack(vc[:, sl])
            state = s_ref[pair]
            ars = _dot_nt(ar, state)
            sa = _dot(t_inv, ars[0:LANES] + _dot(l_ak, v_bd))
            sav = jnp.concatenate([sa, v_bd], axis=0)
            y_bd = ars[LANES:] + _dot(jnp.concatenate([l_rb, l_rk], axis=1), sav)
            o_s[rows, sl] = y_bd[0:CHUNK] + y_bd[CHUNK:]
            upd = _dot_tn(sav, jnp.concatenate([stack(bd[:, sl]), stack(kd[:, sl])], axis=0))
            s_ref[pair] = state * wl[:, sl] + upd
        return carry

    lax.fori_loop(0, tb // CHUNK, chunk_body, 0)

    y = o_s[...]
    mean = _dot_exact_rhs(y, ones_h) * (1.0 / HEAD)
    d = y - mean
    var = _dot_exact_rhs(d * d, ones_h) * (1.0 / HEAD)
    yn = d * lax.rsqrt(var + RW_LN_EPS) * lng_ref[...] + lnb_ref[...]
    bonus = _dot_exact_rhs(r_s[...] * k_s[...] * rk_ref[...], ones_h) * v_s[...]
    y_ref[...] = (yn + bonus) * g


def _seq_spec(tb, c, nt):
    return pl.BlockSpec((tb, c), lambda b, t: (b * nt + t, 0))


def _rwkv(p_rw, params, bsz, seq, tb):
    nt = seq // tb
    n = bsz * seq
    return pl.pallas_call(
        functools.partial(_rwkv_kernel, tb=tb),
        grid=(bsz, nt),
        in_specs=[_seq_spec(tb, p_rw.shape[1], nt)] + [_const_spec(q.shape) for q in params],
        out_specs=_seq_spec(tb, RW_WIDTH, nt),
        out_shape=jax.ShapeDtypeStruct((n, RW_WIDTH), F32),
        scratch_shapes=[pltpu.VMEM((SUBLANES, p_rw.shape[1]), F32),
                        pltpu.VMEM((2, LANES, LANES), F32)]
        + [pltpu.VMEM((tb, RW_WIDTH), F32) for _ in range(7)],
        compiler_params=pltpu.CompilerParams(dimension_semantics=("arbitrary", "arbitrary"),
                                             vmem_limit_bytes=VMEM_LIMIT),
        name="rwkv7",
    )(p_rw, *params)


def _lru_kernel(p_ref, cw_ref, cb_ref, wa_ref, ba_ref, wx_ref, bx_ref, lam_ref, ng_ref, y_ref,
                tail_ref, h_ref, a_s, u_s, *, tb):
    @pl.when(pl.program_id(1) == 0)
    def _():
        tail_ref[...] = jnp.zeros_like(tail_ref)
        h_ref[...] = jnp.zeros_like(h_ref)

    xb = p_ref[:, 0:LRU_WIDTH]
    tail = tail_ref[...]
    xc = cb_ref[...] + xb * cw_ref[LRU_CONV - 1:LRU_CONV, :]
    for j in range(1, LRU_CONV):
        xc = xc + _shift_rows(xb, tail, j) * cw_ref[LRU_CONV - 1 - j:LRU_CONV - j, :]
    tail_ref[...] = xb[tb - SUBLANES:tb, :]

    half = LRU_WIDTH // 2

    def gate(w_ref, b_ref):
        z = jnp.concatenate([_dot(xc[:, 0:half], w_ref[0]), _dot(xc[:, half:], w_ref[1])], axis=1)
        return _sigmoid(z + b_ref[...])

    gate_r = gate(wa_ref, ba_ref)
    gate_i = gate(wx_ref, bx_ref)
    lam = lam_ref[...]
    softplus_neg_lam = jnp.maximum(-lam, 0.0) + jnp.log1p(jnp.exp(-jnp.abs(lam)))
    log_a = (-LRU_C * gate_r) * softplus_neg_lam
    a_s[...] = jnp.exp(log_a)
    th = jnp.tanh(log_a)
    u_s[...] = jnp.sqrt(-2.0 * th / (1.0 - th)) * (gate_i * xc)

    row = _iota((SUBLANES, LRU_WIDTH), 0)

    def scan_body(i, h_prev):
        rows = pl.ds(pl.multiple_of(i * SUBLANES, SUBLANES), SUBLANES)
        a = a_s[rows, :]
        u = u_s[rows, :]
        for d in (1, 2, 4):
            keep = row >= d
            a_sh = jnp.where(keep, pltpu.roll(a, d, 0), 1.0)
            u_sh = jnp.where(keep, pltpu.roll(u, d, 0), 0.0)
            u = a * u_sh + u
            a = a * a_sh
        h = u + a * h_prev
        u_s[rows, :] = h
        return h[SUBLANES - 1:SUBLANES, :]

    h_ref[...] = lax.fori_loop(0, tb // SUBLANES, scan_body, h_ref[...])

    y = u_s[...] * jax.nn.gelu(p_ref[:, LRU_WIDTH:])
    ones_h = _group_ones(half, HEAD)
    ms = jnp.concatenate([_dot_exact_rhs(y[:, 0:half] * y[:, 0:half], ones_h),
                          _dot_exact_rhs(y[:, half:] * y[:, half:], ones_h)], axis=1) * (1.0 / HEAD)
    y_ref[...] = y * lax.rsqrt(ms + NORM_EPS) * ng_ref[...]


def _lru(p_lru, params, bsz, seq, tb):
    nt = seq // tb
    n = bsz * seq
    return pl.pallas_call(
        functools.partial(_lru_kernel, tb=tb),
        grid=(bsz, nt),
        in_specs=[_seq_spec(tb, p_lru.shape[1], nt)] + [_const_spec(q.shape) for q in params],
        out_specs=_seq_spec(tb, LRU_WIDTH, nt),
        out_shape=jax.ShapeDtypeStruct((n, LRU_WIDTH), F32),
        scratch_shapes=[pltpu.VMEM((SUBLANES, LRU_WIDTH), F32), pltpu.VMEM((1, LRU_WIDTH), F32),
                        pltpu.VMEM((tb, LRU_WIDTH), F32), pltpu.VMEM((tb, LRU_WIDTH), F32)],
        compiler_params=pltpu.CompilerParams(dimension_semantics=("arbitrary", "arbitrary"),
                                             vmem_limit_bytes=VMEM_LIMIT),
        name="rglru",
    )(p_lru, *params)


def _gla_kernel(p_ref, gkup_ref, gkb_ref, ng_ref, y_ref,
                s_ref, q_s, k_s, v_s, la_s, o_s, *, tb):
    @pl.when(pl.program_id(1) == 0)
    def _():
        s_ref[...] = jnp.zeros_like(s_ref)

    kw = GLA_KEY_WIDTH
    q_s[...] = p_ref[:, 0:kw] * (GLA_DK ** -0.5)
    k_s[...] = p_ref[:, kw:2 * kw]
    v_s[...] = p_ref[:, 2 * kw:2 * kw + GLA_WIDTH]
    x = _dot3(p_ref[:, 768:GLA_PAD], gkup_ref[...]) + gkb_ref[...]
    la_s[...] = (jnp.minimum(x, 0.0) - jnp.log1p(jnp.exp(-jnp.abs(x)))) * (1.0 / GLA_GATE_NORM)

    nh = GLA_WIDTH // HEAD
    tril = _tril_ones(CHUNK)
    lane_k = _iota((CHUNK, kw), 1) // GLA_DK
    lane_v = _iota((CHUNK, GLA_WIDTH), 1) // HEAD
    causal = (_iota((nh * CHUNK, CHUNK), 0) % CHUNK) >= _iota((nh * CHUNK, CHUNK), 1)
    diag = (_iota((GLA_WIDTH, kw), 0) // HEAD) == (_iota((GLA_WIDTH, kw), 1) // GLA_DK)

    def chunk_body(c, carry):
        rows = pl.ds(pl.multiple_of(c * CHUNK, CHUNK), CHUNK)
        q, k, v, la = q_s[rows, :], k_s[rows, :], v_s[rows, :], la_s[rows, :]
        bc = _dot_exact_lhs(tril, la)
        bl = bc[CHUNK - 1:CHUNK, :]
        qi = q * jnp.exp(bc)
        ki = k * jnp.exp(-bc)
        kd = k * jnp.exp(bl - bc)
        qs = jnp.concatenate([jnp.where(lane_k == h, qi, 0.0) for h in range(nh)], axis=0)
        sc = jnp.where(causal, _dot_nt(qs, ki), 0.0)
        o_full = _dot(sc, v)
        o = _dot_nt(qi, s_ref[...])
        for h in range(nh):
            o = o + jnp.where(lane_v == h, o_full[h * CHUNK:(h + 1) * CHUNK], 0.0)
        o_s[rows, :] = o
        s_ref[...] = s_ref[...] * jnp.exp(bl) + jnp.where(diag, _dot_tn(v, kd), 0.0)
        return carry

    lax.fori_loop(0, tb // CHUNK, chunk_body, 0)

    o = o_s[...]
    ms = _dot_exact_rhs(o * o, _group_ones(GLA_WIDTH, HEAD)) * (1.0 / HEAD)
    gate = p_ref[:, 512:768]
    y_ref[...] = o * lax.rsqrt(ms + NORM_EPS) * ng_ref[...] * (gate * _sigmoid(gate))


def _gla(p_gla, params, bsz, seq, tb):
    nt = seq // tb
    n = bsz * seq
    return pl.pallas_call(
        functools.partial(_gla_kernel, tb=tb),
        grid=(bsz, nt),
        in_specs=[_seq_spec(tb, p_gla.shape[1], nt)] + [_const_spec(q.shape) for q in params],
        out_specs=_seq_spec(tb, GLA_WIDTH, nt),
        out_shape=jax.ShapeDtypeStruct((n, GLA_WIDTH), F32),
        scratch_shapes=[pltpu.VMEM((GLA_WIDTH, GLA_KEY_WIDTH), F32),
                        pltpu.VMEM((tb, GLA_KEY_WIDTH), F32), pltpu.VMEM((tb, GLA_KEY_WIDTH), F32),
                        pltpu.VMEM((tb, GLA_WIDTH), F32), pltpu.VMEM((tb, GLA_KEY_WIDTH), F32),
                        pltpu.VMEM((tb, GLA_WIDTH), F32)],
        compiler_params=pltpu.CompilerParams(dimension_semantics=("arbitrary", "arbitrary"),
                                             vmem_limit_bytes=VMEM_LIMIT),
        name="gla",
    )(p_gla, *params)


def _post_kernel(yrw_ref, ylru_ref, ygla_ref, x_ref, wout_ref, g2_ref, wg_ref, wu_ref, wd_ref,
                 gf_ref, o_ref, *, ff_chunk, final):
    y = jnp.concatenate([yrw_ref[...], ylru_ref[...], ygla_ref[...]], axis=1).astype(BF16)
    x = x_ref[...] + jnp.dot(y, wout_ref[...], preferred_element_type=F32)
    hn = _rms_rows(x, g2_ref[...]).astype(BF16)
    d_ff = wg_ref.shape[1]
    acc = x
    for c in range(d_ff // ff_chunk):
        cols = slice(c * ff_chunk, (c + 1) * ff_chunk)
        gate = jnp.dot(hn, wg_ref[:, cols], preferred_element_type=F32)
        up = jnp.dot(hn, wu_ref[:, cols], preferred_element_type=F32)
        h = (gate * _sigmoid(gate) * up).astype(BF16)
        acc = acc + jnp.dot(h, wd_ref[cols, :], preferred_element_type=F32)
    if final:
        acc = _rms_rows(acc, gf_ref[...])
    o_ref[...] = acc


def _post(y_rw, y_lru, y_gla, xf, w_out, g2, w_gate, w_up, w_down, gf, tm, final):
    n, d = xf.shape
    row = lambda c: pl.BlockSpec((tm, c), lambda i: (i, 0))
    consts = (w_out, g2, w_gate, w_up, w_down, gf)
    return pl.pallas_call(
        functools.partial(_post_kernel, ff_chunk=256, final=final),
        grid=(n // tm,),
        in_specs=[row(y_rw.shape[1]), row(y_lru.shape[1]), row(y_gla.shape[1]), row(d)]
        + [_const_spec(q.shape) for q in consts],
        out_specs=row(d),
        out_shape=jax.ShapeDtypeStruct((n, d), F32),
        compiler_params=pltpu.CompilerParams(dimension_semantics=("arbitrary",),
                                             vmem_limit_bytes=VMEM_LIMIT),
        name="post",
    )(y_rw, y_lru, y_gla, xf, *consts)


def _row(v):
    return v.reshape(1, -1).astype(F32)


def _block_diag_groups(w, per_group):
    nb, c, _ = w.shape
    groups = nb // per_group
    w = w.reshape(groups, per_group, c, c)
    eye = jnp.eye(per_group, dtype=w.dtype)
    out = jnp.einsum("gbij,bk->gbikj", w, eye)
    return out.reshape(groups, per_group * c, per_group * c)


def kernel(x, norm1_g, w_in, rw_mu, rw_w0, rw_w_up, rw_a0, rw_a_up, rw_g_up, rw_k_k, rw_k_a, rw_r_k, rw_ln_g, rw_ln_b, lru_conv_w, lru_conv_b, lru_wa, lru_ba, lru_wx, lru_bx, lru_lam, lru_norm_g, gla_gk_up, gla_gk_b, gla_norm_g, w_out, norm2_g, ffn_w_gate, ffn_w_up, ffn_w_down, final_norm_g):
    bsz, seq, d = x.shape
    depth = w_in.shape[0]
    n = bsz * seq
    tm = 512 if n % 512 == 0 else 256
    tb = 256
    rw_proj = rw_mu.shape[1]
    lru_proj = 2 * LRU_WIDTH
    xf = x.reshape(n, d)
    for l in range(depth):
        w_l = w_in[l]
        w_rw = w_l[:, :rw_proj].astype(BF16)
        w_lru = w_l[:, rw_proj:rw_proj + lru_proj].astype(BF16)
        w_gla = jnp.pad(w_l[:, rw_proj + lru_proj:], ((0, 0), (0, GLA_PAD - (w_l.shape[1] - rw_proj - lru_proj))))
        gq = 2 * GLA_KEY_WIDTH + GLA_WIDTH
        w_gla = jnp.concatenate([w_gla[:, :gq], w_gla[:, gq + GLA_GATE_RANK:gq + GLA_GATE_RANK + GLA_WIDTH],
                                 w_gla[:, gq:gq + GLA_GATE_RANK],
                                 w_gla[:, gq + GLA_GATE_RANK + GLA_WIDTH:]], axis=1).astype(BF16)
        p_rw, p_lru, p_gla = _inproj(xf, _row(norm1_g[l]), w_rw, w_lru, w_gla, tm)

        wup = jnp.pad(rw_w_up[l], ((0, HEAD), (0, 0))).astype(BF16)
        aup = jnp.pad(rw_a_up[l], ((HEAD, 0), (0, 0))).astype(BF16)
        rw_params = (_row(rw_mu[l]), _row(rw_w0[l]), wup, _row(rw_a0[l]), aup, rw_g_up[l].astype(BF16),
                     _row(rw_k_k[l]), _row(rw_k_a[l]), _row(rw_r_k[l]), _row(rw_ln_g[l]), _row(rw_ln_b[l]))
        y_rw = _rwkv(p_rw, rw_params, bsz, seq, tb)

        lru_params = (lru_conv_w[l].astype(F32), _row(lru_conv_b[l]),
                      _block_diag_groups(lru_wa[l], 4).astype(BF16), _row(lru_ba[l]),
                      _block_diag_groups(lru_wx[l], 4).astype(BF16), _row(lru_bx[l]),
                      _row(lru_lam[l]), _row(lru_norm_g[l]))
        y_lru = _lru(p_lru, lru_params, bsz, seq, tb)

        gkup = jnp.pad(gla_gk_up[l], ((0, LANES - GLA_GATE_RANK), (0, 0))).astype(F32)
        gla_params = (gkup, _row(gla_gk_b[l]), _row(jnp.tile(gla_norm_g[l], GLA_WIDTH // HEAD)))
        y_gla = _gla(p_gla, gla_params, bsz, seq, tb)

        xf = _post(y_rw, y_lru, y_gla, xf, w_out[l].astype(BF16), _row(norm2_g[l]),
                   ffn_w_gate[l].astype(BF16), ffn_w_up[l].astype(BF16), ffn_w_down[l].astype(BF16),
                   _row(final_norm_g), tm, final=(l == depth - 1))
    return xf.reshape(bsz, seq, d)
```

```python
import functools
import math

import jax
import jax.numpy as jnp
from jax import lax
from jax.experimental import pallas as pl
from jax.experimental.pallas import tpu as pltpu

F32 = jnp.float32
BF16 = jnp.bfloat16

HEAD = 64
NORM_EPS = 1e-6
RW_WIDTH = 256
RW_DECAY_SCALE = math.exp(-0.5)
RW_LN_EPS = 64e-5
LRU_WIDTH = 512
LRU_CONV = 4
LRU_C = 8.0
GLA_WIDTH = 256
GLA_KEY_WIDTH = 128
GLA_DK = 32
GLA_GATE_RANK = 16
GLA_GATE_NORM = 16.0
GLA_PAD = 896

CHUNK = 64
LANES = 128
SUBLANES = 8
VMEM_LIMIT = 56 * 1024 * 1024


def _dot(a, b):
    return jnp.dot(a.astype(BF16), b.astype(BF16), preferred_element_type=F32)


def _dot_nt(a, b):
    return lax.dot_general(a.astype(BF16), b.astype(BF16), (((1,), (1,)), ((), ())),
                           preferred_element_type=F32)


def _dot_tn(a, b):
    return lax.dot_general(a.astype(BF16), b.astype(BF16), (((0,), (0,)), ((), ())),
                           preferred_element_type=F32)


def _split(x):
    hi = x.astype(BF16)
    lo = (x - hi.astype(F32)).astype(BF16)
    return hi, lo


def _dot_exact_rhs(x, w):
    hi, lo = _split(x)
    return (jnp.dot(hi, w, preferred_element_type=F32)
            + jnp.dot(lo, w, preferred_element_type=F32))


def _dot_exact_lhs(w, x):
    hi, lo = _split(x)
    return (jnp.dot(w, hi, preferred_element_type=F32)
            + jnp.dot(w, lo, preferred_element_type=F32))


def _dot3(a, b):
    ah, al = _split(a)
    bh, bl = _split(b)
    return (jnp.dot(ah, bh, preferred_element_type=F32)
            + jnp.dot(al, bh, preferred_element_type=F32)
            + jnp.dot(ah, bl, preferred_element_type=F32))


def _iota(shape, dim):
    return lax.broadcasted_iota(jnp.int32, shape, dim)


def _group_ones(n, group):
    r = _iota((n, n), 0) // group
    c = _iota((n, n), 1) // group
    return jnp.where(r == c, 1.0, 0.0).astype(BF16)


def _tril_ones(n):
    return jnp.where(_iota((n, n), 0) >= _iota((n, n), 1), 1.0, 0.0).astype(BF16)


def _rms_rows(x, g):
    ms = jnp.mean(x * x, axis=-1, keepdims=True)
    return x * lax.rsqrt(ms + NORM_EPS) * g


def _shift_rows(x, prev_tail, j):
    xs = pltpu.roll(x, j, 0)
    fix = pltpu.roll(prev_tail, j, 0)
    row = _iota((SUBLANES, x.shape[1]), 0)
    head = jnp.where(row < j, fix, xs[0:SUBLANES])
    return jnp.concatenate([head, xs[SUBLANES:]], axis=0)


def _sigmoid(x):
    return 1.0 / (1.0 + jnp.exp(-x))


def _inproj_kernel(x_ref, g_ref, wrw_ref, wlru_ref, wgla_ref, prw_ref, plru_ref, pgla_ref):
    hn = _rms_rows(x_ref[...], g_ref[...]).astype(BF16)
    prw_ref[...] = jnp.dot(hn, wrw_ref[...], preferred_element_type=F32)
    plru_ref[...] = jnp.dot(hn, wlru_ref[...], preferred_element_type=F32)
    pgla_ref[...] = jnp.dot(hn, wgla_ref[...], preferred_element_type=F32)


def _const_spec(shape):
    nd = len(shape)
    return pl.BlockSpec(shape, lambda *_: (0,) * nd)


def _inproj(xf, g, w_rw, w_lru, w_gla, tm):
    n, d = xf.shape
    row = lambda c: pl.BlockSpec((tm, c), lambda i: (i, 0))
    return pl.pallas_call(
        _inproj_kernel,
        grid=(n // tm,),
        in_specs=[row(d), _const_spec(g.shape), _const_spec(w_rw.shape),
                  _const_spec(w_lru.shape), _const_spec(w_gla.shape)],
        out_specs=[row(w_rw.shape[1]), row(w_lru.shape[1]), row(w_gla.shape[1])],
        out_shape=[jax.ShapeDtypeStruct((n, w.shape[1]), F32) for w in (w_rw, w_lru, w_gla)],
        compiler_params=pltpu.CompilerParams(dimension_semantics=("arbitrary",),
                                             vmem_limit_bytes=VMEM_LIMIT),
        name="inproj",
    )(xf, g, w_rw, w_lru, w_gla)


def _rwkv_kernel(p_ref, mu_ref, w0_ref, wup_ref, a0_ref, aup_ref, gup_ref, kk_ref, ka_ref,
                 rk_ref, lng_ref, lnb_ref, y_ref,
                 tail_ref, s_ref, *, tb):
    @pl.when(pl.program_id(1) == 0)
    def _():
        tail_ref[...] = jnp.zeros_like(tail_ref)
        s_ref[...] = jnp.zeros_like(s_ref)

    p = p_ref[...]
    shifted = _shift_rows(p, tail_ref[...], 1)
    tail_ref[...] = p[tb - SUBLANES:tb, :]
    ps = p + (shifted - p) * mu_ref[...]
    r = ps[:, 0:256]
    k = ps[:, 256:512]
    v = ps[:, 512:768]
    lo_wa = ps[:, 768:896]
    g_lo = ps[:, 896:1024]
    w_raw = w0_ref[...] + _dot(jnp.tanh(lo_wa), wup_ref[...])
    lw = -RW_DECAY_SCALE * _sigmoid(w_raw)
    a = _sigmoid(a0_ref[...] + _dot(lo_wa, aup_ref[...]))
    g = _dot(_sigmoid(g_lo), gup_ref[...])
    ones_h = _group_ones(RW_WIDTH, HEAD)
    kk = k * kk_ref[...]
    kk = kk / jnp.maximum(jnp.sqrt(_dot_exact_rhs(kk * kk, ones_h)), 1e-12)
    k = k * (1.0 + (a - 1.0) * ka_ref[...])
    a_neg = -kk
    b = kk * a

    tr = _iota((tb, tb), 0)
    tc = _iota((tb, tb), 1)
    same_chunk = (tr // CHUNK) == (tc // CHUNK)
    cs = _dot_exact_lhs(jnp.where(same_chunk & (tr >= tc), 1.0, 0.0).astype(BF16), lw)
    cl = _dot_exact_lhs(jnp.where(same_chunk, 1.0, 0.0).astype(BF16), lw)
    inv = jnp.exp(-cs)
    dec = jnp.exp(cl - cs)
    wl = jnp.exp(cl)
    at, rt, bt, kt = a_neg * jnp.exp(cs - lw), r * jnp.exp(cs), b * inv, k * inv
    bd, kd = b * dec, k * dec

    rr = _iota((LANES, LANES), 0)
    cc = _iota((LANES, LANES), 1)
    same = (rr // CHUNK) == (cc // CHUNK)
    strict = same & (cc < rr)
    incl = same & (cc <= rr)
    eye = jnp.where(rr == cc, 1.0, 0.0).astype(F32)
    m0 = _iota((CHUNK, LANES), 1) < HEAD
    zeros = jnp.zeros((LANES, LANES), F32)

    def stack(x):
        return jnp.concatenate([jnp.where(m0, x, 0.0), jnp.where(m0, 0.0, x)], axis=0)

    def tile(x):
        return jnp.concatenate([x, x], axis=0)

    nc = tb // CHUNK
    idx = [(c, pair) for c in range(nc) for pair in range(2)]

    def blk(x, c, pair):
        return x[c * CHUNK:(c + 1) * CHUNK, pair * LANES:(pair + 1) * LANES]

    a_st = [stack(blk(at, *i)) for i in idx]
    ar = [jnp.concatenate([a_st[j], stack(blk(rt, *i))], axis=0) for j, i in enumerate(idx)]
    gm = [_dot_nt(ar[j], jnp.concatenate([tile(blk(bt, *i)), tile(blk(kt, *i))], axis=0))
          for j, i in enumerate(idx)]
    n_mat = [jnp.where(strict, m[0:LANES, 0:LANES], 0.0) for m in gm]
    l_ak = [jnp.where(strict, m[0:LANES, LANES:], 0.0) for m in gm]
    l_rbk = [jnp.concatenate([jnp.where(incl, m[LANES:, 0:LANES], 0.0),
                              jnp.where(incl, m[LANES:, LANES:], 0.0)], axis=1) for m in gm]
    v_bd = [stack(blk(v, *i)) for i in idx]
    lv = [_dot(l, vb) for l, vb in zip(l_ak, v_bd)]

    d4 = (rr // 4) == (cc // 4)
    nd = [jnp.where(d4, m, 0.0) for m in n_mat]
    nd2 = [_dot(m, m) for m in nd]
    t_inv = [eye + m + _dot(m2, eye + m) for m, m2 in zip(nd, nd2)]
    s = 4
    while s < CHUNK:
        off = ((rr // (2 * s)) == (cc // (2 * s))) & ((rr // s) != (cc // s))
        nt = [_dot(jnp.where(off, m, 0.0), t) for m, t in zip(n_mat, t_inv)]
        t_inv = [t + _dot(t, m) for t, m in zip(t_inv, nt)]
        s *= 2

    x = [_dot(t, jnp.concatenate([a, l], axis=1)) for t, a, l in zip(t_inv, a_st, lv)]
    z = [jnp.concatenate([xx, jnp.concatenate([zeros, vb], axis=1)], axis=0) for xx, vb in zip(x, v_bd)]
    y2 = [_dot(l, zz) for l, zz in zip(l_rbk, z)]
    ms = [_dot_tn(zz, jnp.concatenate([stack(blk(bd, *i)), stack(blk(kd, *i))], axis=0))
          for zz, i in zip(z, idx)]

    outs = []
    states = [s_ref[0], s_ref[1]]
    for c in range(nc):
        halves = []
        for pair in range(2):
            j = 2 * c + pair
            state = states[pair]
            y_bd = _dot_nt(ar[j][LANES:] + y2[j][:, 0:LANES], state) + y2[j][:, LANES:]
            halves.append(y_bd[0:CHUNK] + y_bd[CHUNK:])
            states[pair] = (state * blk(wl, c, pair)[0:1] + _dot(state, ms[j][0:LANES])
                            + ms[j][LANES:])
        outs.append(jnp.concatenate(halves, axis=1))
    s_ref[0] = states[0]
    s_ref[1] = states[1]

    y = jnp.concatenate(outs, axis=0)
    mean = _dot_exact_rhs(y, ones_h) * (1.0 / HEAD)
    d = y - mean
    var = _dot_exact_rhs(d * d, ones_h) * (1.0 / HEAD)
    yn = d * lax.rsqrt(var + RW_LN_EPS) * lng_ref[...] + lnb_ref[...]
    bonus = _dot_exact_rhs(r * k * rk_ref[...], ones_h) * v
    y_ref[...] = (yn + bonus) * g


def _seq_spec(tb, c, nt):
    return pl.BlockSpec((tb, c), lambda b, t: (b * nt + t, 0))


def _rwkv(p_rw, params, bsz, seq, tb):
    nt = seq // tb
    n = bsz * seq
    return pl.pallas_call(
        functools.partial(_rwkv_kernel, tb=tb),
        grid=(bsz, nt),
        in_specs=[_seq_spec(tb, p_rw.shape[1], nt)] + [_const_spec(q.shape) for q in params],
        out_specs=_seq_spec(tb, RW_WIDTH, nt),
        out_shape=jax.ShapeDtypeStruct((n, RW_WIDTH), F32),
        scratch_shapes=[pltpu.VMEM((SUBLANES, p_rw.shape[1]), F32),
                        pltpu.VMEM((2, LANES, LANES), F32)],
        compiler_params=pltpu.CompilerParams(dimension_semantics=("arbitrary", "arbitrary"),
                                             vmem_limit_bytes=VMEM_LIMIT),
        name="rwkv7",
    )(p_rw, *params)


def _lru_kernel(p_ref, cw_ref, cb_ref, wa_ref, ba_ref, wx_ref, bx_ref, lam_ref, ng_ref, y_ref,
                tail_ref, h_ref, *, tb):
    @pl.when(pl.program_id(1) == 0)
    def _():
        tail_ref[...] = jnp.zeros_like(tail_ref)
        h_ref[...] = jnp.zeros_like(h_ref)

    xb = p_ref[:, 0:LRU_WIDTH]
    tail = tail_ref[...]
    xc = cb_ref[...] + xb * cw_ref[LRU_CONV - 1:LRU_CONV, :]
    for j in range(1, LRU_CONV):
        xc = xc + _shift_rows(xb, tail, j) * cw_ref[LRU_CONV - 1 - j:LRU_CONV - j, :]
    tail_ref[...] = xb[tb - SUBLANES:tb, :]

    half = LRU_WIDTH // 2

    def gate(w_ref, b_ref):
        z = jnp.concatenate([_dot(xc[:, 0:half], w_ref[0]), _dot(xc[:, half:], w_ref[1])], axis=1)
        return _sigmoid(z + b_ref[...])

    gate_r = gate(wa_ref, ba_ref)
    gate_i = gate(wx_ref, bx_ref)
    lam = lam_ref[...]
    softplus_neg_lam = jnp.maximum(-lam, 0.0) + jnp.log1p(jnp.exp(-jnp.abs(lam)))
    log_a = (-LRU_C * gate_r) * softplus_neg_lam
    a = jnp.exp(log_a)
    th = jnp.tanh(log_a)
    u = jnp.sqrt(-2.0 * th / (1.0 - th)) * (gate_i * xc)

    row = _iota((tb, LRU_WIDTH), 0) % SUBLANES
    for d in (1, 2, 4):
        keep = row >= d
        a_sh = jnp.where(keep, pltpu.roll(a, d, 0), 1.0)
        u_sh = jnp.where(keep, pltpu.roll(u, d, 0), 0.0)
        u = a * u_sh + u
        a = a * a_sh
    h_prev = h_ref[...]
    hs = []
    for i in range(tb // SUBLANES):
        rows = slice(i * SUBLANES, (i + 1) * SUBLANES)
        h = u[rows] + a[rows] * h_prev
        hs.append(h)
        h_prev = h[SUBLANES - 1:SUBLANES, :]
    h_ref[...] = h_prev

    y = jnp.concatenate(hs, axis=0) * jax.nn.gelu(p_ref[:, LRU_WIDTH:])
    ones_h = _group_ones(half, HEAD)
    ms = jnp.concatenate([_dot_exact_rhs(y[:, 0:half] * y[:, 0:half], ones_h),
                          _dot_exact_rhs(y[:, half:] * y[:, half:], ones_h)], axis=1) * (1.0 / HEAD)
    y_ref[...] = y * lax.rsqrt(ms + NORM_EPS) * ng_ref[...]


def _lru(p_lru, params, bsz, seq, tb):
    nt = seq // tb
    n = bsz * seq
    return pl.pallas_call(
        functools.partial(_lru_kernel, tb=tb),
        grid=(bsz, nt),
        in_specs=[_seq_spec(tb, p_lru.shape[1], nt)] + [_const_spec(q.shape) for q in params],
        out_specs=_seq_spec(tb, LRU_WIDTH, nt),
        out_shape=jax.ShapeDtypeStruct((n, LRU_WIDTH), F32),
        scratch_shapes=[pltpu.VMEM((SUBLANES, LRU_WIDTH), F32), pltpu.VMEM((1, LRU_WIDTH), F32)],
        compiler_params=pltpu.CompilerParams(dimension_semantics=("arbitrary", "arbitrary"),
                                             vmem_limit_bytes=VMEM_LIMIT),
        name="rglru",
    )(p_lru, *params)


def _gla_kernel(p_ref, gkup_ref, gkb_ref, ng_ref, y_ref,
                s_ref, *, tb):
    @pl.when(pl.program_id(1) == 0)
    def _():
        s_ref[...] = jnp.zeros_like(s_ref)

    kw = GLA_KEY_WIDTH
    q = p_ref[:, 0:kw] * (GLA_DK ** -0.5)
    k = p_ref[:, kw:2 * kw]
    v = p_ref[:, 2 * kw:2 * kw + GLA_WIDTH]
    x = _dot3(p_ref[:, 768:GLA_PAD], gkup_ref[...]) + gkb_ref[...]
    la = (jnp.minimum(x, 0.0) - jnp.log1p(jnp.exp(-jnp.abs(x)))) * (1.0 / GLA_GATE_NORM)

    tr = _iota((tb, tb), 0)
    tc = _iota((tb, tb), 1)
    same_chunk = (tr // CHUNK) == (tc // CHUNK)
    bc = _dot_exact_lhs(jnp.where(same_chunk & (tr >= tc), 1.0, 0.0).astype(BF16), la)
    bl = _dot_exact_lhs(jnp.where(same_chunk, 1.0, 0.0).astype(BF16), la)
    qi = q * jnp.exp(bc)
    ki = k * jnp.exp(-bc)
    kd = k * jnp.exp(bl - bc)
    dec = jnp.exp(bl)

    nh = GLA_WIDTH // HEAD
    lane_k = _iota((tb, kw), 1) // GLA_DK
    lane_v = _iota((tb, GLA_WIDTH), 1) // HEAD
    qs = jnp.concatenate([jnp.where(lane_k == h, qi, 0.0) for h in range(nh)], axis=0)
    st = _iota((nh * tb, tb), 0) % tb
    sj = _iota((nh * tb, tb), 1)
    causal = ((st // CHUNK) == (sj // CHUNK)) & (st >= sj)
    sc = jnp.where(causal, _dot_nt(qs, ki), 0.0)
    o_full = _dot(sc, v)
    o_intra = jnp.where(lane_v == 0, o_full[0:tb], 0.0)
    for h in range(1, nh):
        o_intra = o_intra + jnp.where(lane_v == h, o_full[h * tb:(h + 1) * tb], 0.0)

    diag = (_iota((GLA_WIDTH, kw), 0) // HEAD) == (_iota((GLA_WIDTH, kw), 1) // GLA_DK)
    nc = tb // CHUNK
    upd = [jnp.where(diag, _dot_tn(v[c * CHUNK:(c + 1) * CHUNK], kd[c * CHUNK:(c + 1) * CHUNK]), 0.0)
           for c in range(nc)]
    state = s_ref[...]
    outs = []
    for c in range(nc):
        rows = slice(c * CHUNK, (c + 1) * CHUNK)
        outs.append(o_intra[rows] + _dot_nt(qi[rows], state))
        state = state * dec[c * CHUNK:c * CHUNK + 1, :] + upd[c]
    s_ref[...] = state

    o = jnp.concatenate(outs, axis=0)
    ms = _dot_exact_rhs(o * o, _group_ones(GLA_WIDTH, HEAD)) * (1.0 / HEAD)
    gate = p_ref[:, 512:768]
    y_ref[...] = o * lax.rsqrt(ms + NORM_EPS) * ng_ref[...] * (gate * _sigmoid(gate))


def _gla(p_gla, params, bsz, seq, tb):
    nt = seq // tb
    n = bsz * seq
    return pl.pallas_call(
        functools.partial(_gla_kernel, tb=tb),
        grid=(bsz, nt),
        in_specs=[_seq_spec(tb, p_gla.shape[1], nt)] + [_const_spec(q.shape) for q in params],
        out_specs=_seq_spec(tb, GLA_WIDTH, nt),
        out_shape=jax.ShapeDtypeStruct((n, GLA_WIDTH), F32),
        scratch_shapes=[pltpu.VMEM((GLA_WIDTH, GLA_KEY_WIDTH), F32)],
        compiler_params=pltpu.CompilerParams(dimension_semantics=("arbitrary", "arbitrary"),
                                             vmem_limit_bytes=VMEM_LIMIT),
        name="gla",
    )(p_gla, *params)


def _post_kernel(yrw_ref, ylru_ref, ygla_ref, x_ref, wout_ref, g2_ref, wg_ref, wu_ref, wd_ref,
                 gf_ref, o_ref, *, ff_chunk, final):
    y = jnp.concatenate([yrw_ref[...], ylru_ref[...], ygla_ref[...]], axis=1).astype(BF16)
    x = x_ref[...] + jnp.dot(y, wout_ref[...], preferred_element_type=F32)
    hn = _rms_rows(x, g2_ref[...]).astype(BF16)
    d_ff = wg_ref.shape[1]
    acc = x
    for c in range(d_ff // ff_chunk):
        cols = slice(c * ff_chunk, (c + 1) * ff_chunk)
        gate = jnp.dot(hn, wg_ref[:, cols], preferred_element_type=F32)
        up = jnp.dot(hn, wu_ref[:, cols], preferred_element_type=F32)
        h = (gate * _sigmoid(gate) * up).astype(BF16)
        acc = acc + jnp.dot(h, wd_ref[cols, :], preferred_element_type=F32)
    if final:
        acc = _rms_rows(acc, gf_ref[...])
    o_ref[...] = acc


def _post(y_rw, y_lru, y_gla, xf, w_out, g2, w_gate, w_up, w_down, gf, tm, final):
    n, d = xf.shape
    row = lambda c: pl.BlockSpec((tm, c), lambda i: (i, 0))
    consts = (w_out, g2, w_gate, w_up, w_down, gf)
    return pl.pallas_call(
        functools.partial(_post_kernel, ff_chunk=256, final=final),
        grid=(n // tm,),
        in_specs=[row(y_rw.shape[1]), row(y_lru.shape[1]), row(y_gla.shape[1]), row(d)]
        + [_const_spec(q.shape) for q in consts],
        out_specs=row(d),
        out_shape=jax.ShapeDtypeStruct((n, d), F32),
        compiler_params=pltpu.CompilerParams(dimension_semantics=("arbitrary",),
                                             vmem_limit_bytes=VMEM_LIMIT),
        name="post",
    )(y_rw, y_lru, y_gla, xf, *consts)


def _row(v):
    return v.reshape(1, -1).astype(F32)


def _block_diag_groups(w, per_group):
    nb, c, _ = w.shape
    groups = nb // per_group
    w = w.reshape(groups, per_group, c, c)
    eye = jnp.eye(per_group, dtype=w.dtype)
    out = jnp.einsum("gbij,bk->gbikj", w, eye)
    return out.reshape(groups, per_group * c, per_group * c)


def kernel(x, norm1_g, w_in, rw_mu, rw_w0, rw_w_up, rw_a0, rw_a_up, rw_g_up, rw_k_k, rw_k_a, rw_r_k, rw_ln_g, rw_ln_b, lru_conv_w, lru_conv_b, lru_wa, lru_ba, lru_wx, lru_bx, lru_lam, lru_norm_g, gla_gk_up, gla_gk_b, gla_norm_g, w_out, norm2_g, ffn_w_gate, ffn_w_up, ffn_w_down, final_norm_g):
    bsz, seq, d = x.shape
    depth = w_in.shape[0]
    n = bsz * seq
    tm = 512 if n % 512 == 0 else 256
    tb = 256
    rw_proj = rw_mu.shape[1]
    lru_proj = 2 * LRU_WIDTH
    xf = x.reshape(n, d)
    for l in range(depth):
        w_l = w_in[l]
        w_rw = w_l[:, :rw_proj].astype(BF16)
        w_lru = w_l[:, rw_proj:rw_proj + lru_proj].astype(BF16)
        w_gla = jnp.pad(w_l[:, rw_proj + lru_proj:], ((0, 0), (0, GLA_PAD - (w_l.shape[1] - rw_proj - lru_proj))))
        gq = 2 * GLA_KEY_WIDTH + GLA_WIDTH
        w_gla = jnp.concatenate([w_gla[:, :gq], w_gla[:, gq + GLA_GATE_RANK:gq + GLA_GATE_RANK + GLA_WIDTH],
                                 w_gla[:, gq:gq + GLA_GATE_RANK],
                                 w_gla[:, gq + GLA_GATE_RANK + GLA_WIDTH:]], axis=1).astype(BF16)
        p_rw, p_lru, p_gla = _inproj(xf, _row(norm1_g[l]), w_rw, w_lru, w_gla, tm)

        wup = jnp.pad(rw_w_up[l], ((0, HEAD), (0, 0))).astype(BF16)
        aup = jnp.pad(rw_a_up[l], ((HEAD, 0), (0, 0))).astype(BF16)
        rw_params = (_row(rw_mu[l]), _row(rw_w0[l]), wup, _row(rw_a0[l]), aup, rw_g_up[l].astype(BF16),
                     _row(rw_k_k[l]), _row(rw_k_a[l]), _row(rw_r_k[l]), _row(rw_ln_g[l]), _row(rw_ln_b[l]))
        y_rw = _rwkv(p_rw, rw_params, bsz, seq, tb)

        lru_params = (lru_conv_w[l].astype(F32), _row(lru_conv_b[l]),
                      _block_diag_groups(lru_wa[l], 4).astype(BF16), _row(lru_ba[l]),
                      _block_diag_groups(lru_wx[l], 4).astype(BF16), _row(lru_bx[l]),
                      _row(lru_lam[l]), _row(lru_norm_g[l]))
        y_lru = _lru(p_lru, lru_params, bsz, seq, tb)

        gkup = jnp.pad(gla_gk_up[l], ((0, LANES - GLA_GATE_RANK), (0, 0))).astype(F32)
        gla_params = (gkup, _row(gla_gk_b[l]), _row(jnp.tile(gla_norm_g[l], GLA_WIDTH // HEAD)))
        y_gla = _gla(p_gla, gla_params, bsz, seq, tb)

        xf = _post(y_rw, y_lru, y_gla, xf, w_out[l].astype(BF16), _row(norm2_g[l]),
                   ffn_w_gate[l].astype(BF16), ffn_w_up[l].astype(BF16), ffn_w_down[l].astype(BF16),
                   _row(final_norm_g), tm, final=(l == depth - 1))
    return xf.reshape(bsz, seq, d)
```

```python
import functools
import math

import jax
import jax.numpy as jnp
from jax import lax
from jax.experimental import pallas as pl
from jax.experimental.pallas import tpu as pltpu

F32 = jnp.float32
BF16 = jnp.bfloat16

HEAD = 64
NORM_EPS = 1e-6
RW_WIDTH = 256
RW_DECAY_SCALE = math.exp(-0.5)
RW_LN_EPS = 64e-5
LRU_WIDTH = 512
LRU_CONV = 4
LRU_C = 8.0
GLA_WIDTH = 256
GLA_KEY_WIDTH = 128
GLA_DK = 32
GLA_GATE_RANK = 16
GLA_GATE_NORM = 16.0
GLA_PAD = 896

CHUNK = 64
LANES = 128
SUBLANES = 8
VMEM_LIMIT = 56 * 1024 * 1024


def _dot(a, b):
    return jnp.dot(a.astype(BF16), b.astype(BF16), preferred_element_type=F32)


def _dot_nt(a, b):
    return lax.dot_general(a.astype(BF16), b.astype(BF16), (((1,), (1,)), ((), ())),
                           preferred_element_type=F32)


def _dot_tn(a, b):
    return lax.dot_general(a.astype(BF16), b.astype(BF16), (((0,), (0,)), ((), ())),
                           preferred_element_type=F32)


def _split(x):
    hi = x.astype(BF16)
    lo = (x - hi.astype(F32)).astype(BF16)
    return hi, lo


def _group_sum(x, ones):
    return jnp.dot(x.astype(BF16), ones, preferred_element_type=F32)


def _chunk_cumsums(tril_c, ones_c, x):
    hi, lo = _split(x)
    cs = jnp.dot(tril_c, hi, preferred_element_type=F32) + jnp.dot(tril_c, lo, preferred_element_type=F32)
    cl = jnp.dot(ones_c, hi, preferred_element_type=F32) + jnp.dot(ones_c, lo, preferred_element_type=F32)
    return cs, cl


def _dot3(a, b):
    ah, al = _split(a)
    bh, bl = _split(b)
    return (jnp.dot(ah, bh, preferred_element_type=F32)
            + jnp.dot(al, bh, preferred_element_type=F32)
            + jnp.dot(ah, bl, preferred_element_type=F32))


def _iota(shape, dim):
    return lax.broadcasted_iota(jnp.int32, shape, dim)


def _group_ones(n, group):
    r = _iota((n, n), 0) // group
    c = _iota((n, n), 1) // group
    return jnp.where(r == c, 1.0, 0.0).astype(BF16)


def _chunk_sum_ones(tb):
    tr = _iota((tb, tb), 0)
    tc = _iota((tb, tb), 1)
    same = (tr // CHUNK) == (tc // CHUNK)
    return (jnp.where(same & (tr >= tc), 1.0, 0.0).astype(BF16),
            jnp.where(same, 1.0, 0.0).astype(BF16))


def _rms_rows(x, g):
    ms = jnp.mean(x * x, axis=-1, keepdims=True)
    return x * lax.rsqrt(ms + NORM_EPS) * g


def _shift_rows(x, prev_tail, j):
    xs = pltpu.roll(x, j, 0)
    fix = pltpu.roll(prev_tail, j, 0)
    row = _iota((SUBLANES, x.shape[1]), 0)
    head = jnp.where(row < j, fix, xs[0:SUBLANES])
    return jnp.concatenate([head, xs[SUBLANES:]], axis=0)


def _sigmoid(x):
    return 1.0 / (1.0 + jnp.exp2(x * (-math.log2(math.e))))


def _sqrt_nonneg(y):
    return jnp.where(y > 0.0, y * lax.rsqrt(y), 0.0)


def _const_spec(shape):
    nd = len(shape)
    return pl.BlockSpec(shape, lambda *_: (0,) * nd)


def _inproj_kernel(x_ref, g_ref, wrw_ref, wlru_ref, wgla_ref, prw_ref, plru_ref, pgla_ref):
    hn = _rms_rows(x_ref[...], g_ref[...]).astype(BF16)
    prw_ref[...] = jnp.dot(hn, wrw_ref[...], preferred_element_type=F32)
    plru_ref[...] = jnp.dot(hn, wlru_ref[...], preferred_element_type=F32)
    pgla_ref[...] = jnp.dot(hn, wgla_ref[...], preferred_element_type=F32)


def _inproj(xf, g, w_rw, w_lru, w_gla, tm):
    n, d = xf.shape
    row = lambda c: pl.BlockSpec((tm, c), lambda i: (i, 0))
    return pl.pallas_call(
        _inproj_kernel,
        grid=(n // tm,),
        in_specs=[row(d), _const_spec(g.shape), _const_spec(w_rw.shape),
                  _const_spec(w_lru.shape), _const_spec(w_gla.shape)],
        out_specs=[row(w_rw.shape[1]), row(w_lru.shape[1]), row(w_gla.shape[1])],
        out_shape=[jax.ShapeDtypeStruct((n, w.shape[1]), F32) for w in (w_rw, w_lru, w_gla)],
        compiler_params=pltpu.CompilerParams(dimension_semantics=("arbitrary",),
                                             vmem_limit_bytes=VMEM_LIMIT),
        name="inproj",
    )(xf, g, w_rw, w_lru, w_gla)


def _rwkv_steps(p_ref, prm, y_ref, tail_ref, s_ref, tb):
    (mu_ref, w0_ref, wup_ref, a0_ref, aup_ref, gup_ref, kk_ref, ka_ref, rk_ref, lng_ref,
     lnb_ref) = prm
    p = p_ref[...]
    shifted = _shift_rows(p, tail_ref[...], 1)
    tail_ref[...] = p[tb - SUBLANES:tb, :]
    ps = p + (shifted - p) * mu_ref[...]
    r = ps[:, 0:256]
    k = ps[:, 256:512]
    v = ps[:, 512:768]
    lo_wa = ps[:, 768:896]
    g_lo = ps[:, 896:1024]
    ones_h = _group_ones(RW_WIDTH, HEAD)
    kk = k * kk_ref[...]
    w_raw = w0_ref[...] + _dot(jnp.tanh(lo_wa), wup_ref[...])
    a = _sigmoid(a0_ref[...] + _dot(lo_wa, aup_ref[...]))
    g = _dot(_sigmoid(g_lo), gup_ref[...])
    kk_sq = _group_sum(kk * kk, ones_h)
    yield
    lw = -RW_DECAY_SCALE * _sigmoid(w_raw)
    kk = kk * lax.rsqrt(jnp.maximum(kk_sq, 1e-24))
    k = k * (1.0 + (a - 1.0) * ka_ref[...])
    a_neg = -kk
    b = kk * a

    tril_c, ones_c = _chunk_sum_ones(tb)
    cs, cl = _chunk_cumsums(tril_c, ones_c, lw)
    yield
    inv = jnp.exp(-cs)
    dec = jnp.exp(cl - cs)
    wl = jnp.exp(cl)
    at, rt, bt, kt = a_neg * jnp.exp(cs - lw), r * jnp.exp(cs), b * inv, k * inv
    bd, kd = b * dec, k * dec

    rr = _iota((LANES, LANES), 0)
    cc = _iota((LANES, LANES), 1)
    same = (rr // CHUNK) == (cc // CHUNK)
    strict = same & (cc < rr)
    incl = same & (cc <= rr)
    eye = jnp.where(rr == cc, 1.0, 0.0).astype(F32)
    m0 = _iota((CHUNK, LANES), 1) < HEAD
    zeros = jnp.zeros((LANES, LANES), F32)

    def stack(x):
        return jnp.concatenate([jnp.where(m0, x, 0.0), jnp.where(m0, 0.0, x)], axis=0)

    def tile(x):
        return jnp.concatenate([x, x], axis=0)

    nc = tb // CHUNK
    idx = [(c, pair) for c in range(nc) for pair in range(2)]

    def blk(x, c, pair):
        return x[c * CHUNK:(c + 1) * CHUNK, pair * LANES:(pair + 1) * LANES]

    a_st = [stack(blk(at, *i)) for i in idx]
    ar = [jnp.concatenate([a_st[j], stack(blk(rt, *i))], axis=0) for j, i in enumerate(idx)]
    gm = [_dot_nt(ar[j], jnp.concatenate([tile(blk(bt, *i)), tile(blk(kt, *i))], axis=0))
          for j, i in enumerate(idx)]
    yield
    n_mat = [jnp.where(strict, m[0:LANES, 0:LANES], 0.0) for m in gm]
    l_ak = [jnp.where(strict, m[0:LANES, LANES:], 0.0) for m in gm]
    l_rbk = [jnp.concatenate([jnp.where(incl, m[LANES:, 0:LANES], 0.0),
                              jnp.where(incl, m[LANES:, LANES:], 0.0)], axis=1) for m in gm]
    v_bd = [stack(blk(v, *i)) for i in idx]
    lv = [_dot(l, vb) for l, vb in zip(l_ak, v_bd)]

    d4 = (rr // 4) == (cc // 4)
    nd = [jnp.where(d4, m, 0.0) for m in n_mat]
    nd2 = [_dot(m, m) for m in nd]
    yield
    t_inv = [eye + m + _dot(m2, eye + m) for m, m2 in zip(nd, nd2)]
    yield
    s = 4
    while s < CHUNK:
        off = ((rr // (2 * s)) == (cc // (2 * s))) & ((rr // s) != (cc // s))
        nt = [_dot(jnp.where(off, m, 0.0), t) for m, t in zip(n_mat, t_inv)]
        yield
        t_inv = [t + _dot(t, m) for t, m in zip(t_inv, nt)]
        yield
        s *= 2

    x = [_dot(t, jnp.concatenate([a_, l], axis=1)) for t, a_, l in zip(t_inv, a_st, lv)]
    yield
    z = [jnp.concatenate([xx, jnp.concatenate([zeros, vb], axis=1)], axis=0) for xx, vb in zip(x, v_bd)]
    y2 = [_dot(l, zz) for l, zz in zip(l_rbk, z)]
    ms = [_dot_tn(zz, jnp.concatenate([stack(blk(bd, *i)), stack(blk(kd, *i))], axis=0))
          for zz, i in zip(z, idx)]
    yield

    outs = []
    states = [s_ref[0], s_ref[1]]
    for c in range(nc):
        halves = []
        for pair in range(2):
            j = 2 * c + pair
            state = states[pair]
            y_bd = _dot_nt(ar[j][LANES:] + y2[j][:, 0:LANES], state) + y2[j][:, LANES:]
            halves.append(y_bd[0:CHUNK] + y_bd[CHUNK:])
            states[pair] = (state * blk(wl, c, pair)[0:1] + _dot(state, ms[j][0:LANES])
                            + ms[j][LANES:])
        outs.append(jnp.concatenate(halves, axis=1))
        yield
    s_ref[0] = states[0]
    s_ref[1] = states[1]

    y = jnp.concatenate(outs, axis=0)
    mean = _group_sum(y, ones_h) * (1.0 / HEAD)
    bonus_s = _group_sum(r * k * rk_ref[...], ones_h)
    yield
    d = y - mean
    var = _group_sum(d * d, ones_h) * (1.0 / HEAD)
    yield
    yn = d * lax.rsqrt(var + RW_LN_EPS) * lng_ref[...] + lnb_ref[...]
    y_ref[...] = (yn + bonus_s * v) * g


def _lru_steps(p_ref, prm, y_ref, tail_ref, h_ref, tb):
    cw_ref, cb_ref, wa_ref, ba_ref, wx_ref, bx_ref, lam_ref, ng_ref = prm
    xb = p_ref[:, 0:LRU_WIDTH]
    tail = tail_ref[...]
    xc = cb_ref[...] + xb * cw_ref[LRU_CONV - 1:LRU_CONV, :]
    for j in range(1, LRU_CONV):
        xc = xc + _shift_rows(xb, tail, j) * cw_ref[LRU_CONV - 1 - j:LRU_CONV - j, :]
    tail_ref[...] = xb[tb - SUBLANES:tb, :]

    half = LRU_WIDTH // 2
    z_r = jnp.concatenate([_dot(xc[:, 0:half], wa_ref[0]), _dot(xc[:, half:], wa_ref[1])], axis=1)
    z_i = jnp.concatenate([_dot(xc[:, 0:half], wx_ref[0]), _dot(xc[:, half:], wx_ref[1])], axis=1)
    yield
    gate_r = _sigmoid(z_r + ba_ref[...])
    gate_i = _sigmoid(z_i + bx_ref[...])
    lam = lam_ref[...]
    softplus_neg_lam = jnp.maximum(-lam, 0.0) + jnp.log1p(jnp.exp(-jnp.abs(lam)))
    log_a = (-LRU_C * gate_r) * softplus_neg_lam
    a = jnp.exp(log_a)
    th = jnp.tanh(log_a)
    u = _sqrt_nonneg(-2.0 * th / (1.0 - th)) * (gate_i * xc)

    row = _iota((tb, LRU_WIDTH), 0) % SUBLANES
    for d in (1, 2, 4):
        keep = row >= d
        a_sh = jnp.where(keep, pltpu.roll(a, d, 0), 1.0)
        u_sh = jnp.where(keep, pltpu.roll(u, d, 0), 0.0)
        u = a * u_sh + u
        a = a * a_sh
    h_prev = h_ref[...]
    hs = []
    for i in range(tb // SUBLANES):
        rows = slice(i * SUBLANES, (i + 1) * SUBLANES)
        h = u[rows] + a[rows] * h_prev
        hs.append(h)
        h_prev = h[SUBLANES - 1:SUBLANES, :]
    h_ref[...] = h_prev

    y = jnp.concatenate(hs, axis=0) * jax.nn.gelu(p_ref[:, LRU_WIDTH:])
    ones_h = _group_ones(half, HEAD)
    ysq = y * y
    ms = jnp.concatenate([_group_sum(ysq[:, 0:half], ones_h),
                          _group_sum(ysq[:, half:], ones_h)], axis=1) * (1.0 / HEAD)
    yield
    y_ref[...] = y * lax.rsqrt(ms + NORM_EPS) * ng_ref[...]


def _gla_steps(p_ref, prm, y_ref, s_ref, tb):
    gkup_ref, gkb_ref, ng_ref = prm
    kw = GLA_KEY_WIDTH
    q = p_ref[:, 0:kw] * (GLA_DK ** -0.5)
    k = p_ref[:, kw:2 * kw]
    v = p_ref[:, 2 * kw:2 * kw + GLA_WIDTH]
    x = _dot3(p_ref[:, 768:GLA_PAD], gkup_ref[...]) + gkb_ref[...]
    yield
    la = (jnp.minimum(x, 0.0) - jnp.log1p(jnp.exp(-jnp.abs(x)))) * (1.0 / GLA_GATE_NORM)

    tril_c, ones_c = _chunk_sum_ones(tb)
    bc, bl = _chunk_cumsums(tril_c, ones_c, la)
    yield
    qi = q * jnp.exp(bc)
    ki = k * jnp.exp(-bc)
    kd = k * jnp.exp(bl - bc)
    dec = jnp.exp(bl)

    nh = GLA_WIDTH // HEAD
    lane_k = _iota((tb, kw), 1) // GLA_DK
    lane_v = _iota((tb, GLA_WIDTH), 1) // HEAD
    qs = jnp.concatenate([jnp.where(lane_k == h, qi, 0.0) for h in range(nh)], axis=0)
    st = _iota((nh * tb, tb), 0) % tb
    sj = _iota((nh * tb, tb), 1)
    causal = ((st // CHUNK) == (sj // CHUNK)) & (st >= sj)
    sc = _dot_nt(qs, ki)
    diag = (_iota((GLA_WIDTH, kw), 0) // HEAD) == (_iota((GLA_WIDTH, kw), 1) // GLA_DK)
    nc = tb // CHUNK
    upd = [jnp.where(diag, _dot_tn(v[c * CHUNK:(c + 1) * CHUNK], kd[c * CHUNK:(c + 1) * CHUNK]), 0.0)
           for c in range(nc)]
    yield
    o_full = _dot(jnp.where(causal, sc, 0.0), v)
    yield
    o_intra = jnp.where(lane_v == 0, o_full[0:tb], 0.0)
    for h in range(1, nh):
        o_intra = o_intra + jnp.where(lane_v == h, o_full[h * tb:(h + 1) * tb], 0.0)

    state = s_ref[...]
    outs = []
    for c in range(nc):
        rows = slice(c * CHUNK, (c + 1) * CHUNK)
        outs.append(o_intra[rows] + _dot_nt(qi[rows], state))
        state = state * dec[c * CHUNK:c * CHUNK + 1, :] + upd[c]
    s_ref[...] = state

    o = jnp.concatenate(outs, axis=0)
    ms = _group_sum(o * o, _group_ones(GLA_WIDTH, HEAD)) * (1.0 / HEAD)
    yield
    gate = p_ref[:, 512:768]
    y_ref[...] = o * lax.rsqrt(ms + NORM_EPS) * ng_ref[...] * (gate * _sigmoid(gate))


N_RW_PARAMS = 11
N_LRU_PARAMS = 8
N_GLA_PARAMS = 3


def _mixers_kernel(*refs, tb, nseq):
    prw_ref, plru_ref, pgla_ref = refs[0:3]
    o = 3
    rw_prm = refs[o:o + N_RW_PARAMS]
    o += N_RW_PARAMS
    lru_prm = refs[o:o + N_LRU_PARAMS]
    o += N_LRU_PARAMS
    gla_prm = refs[o:o + N_GLA_PARAMS]
    o += N_GLA_PARAMS
    yrw_ref, ylru_ref, ygla_ref = refs[o:o + 3]
    rw_tail, rw_state, lru_tail, lru_h, gla_state = refs[o + 3:]

    @pl.when(pl.program_id(1) == 0)
    def _():
        for ref in (rw_tail, rw_state, lru_tail, lru_h, gla_state):
            ref[...] = jnp.zeros_like(ref)

    gens = []
    for q in range(nseq):
        gens.append(_rwkv_steps(prw_ref.at[q], rw_prm, yrw_ref.at[q], rw_tail.at[q], rw_state.at[q], tb))
        gens.append(_gla_steps(pgla_ref.at[q], gla_prm, ygla_ref.at[q], gla_state.at[q], tb))
        gens.append(_lru_steps(plru_ref.at[q], lru_prm, ylru_ref.at[q], lru_tail.at[q], lru_h.at[q], tb))
    while gens:
        alive = []
        for gen in gens:
            try:
                next(gen)
                alive.append(gen)
            except StopIteration:
                pass
        gens = alive


def _mixers(p_rw, p_lru, p_gla, rw_params, lru_params, gla_params, bsz, seq, tb, nseq):
    nt = seq // tb
    blk = lambda c: pl.BlockSpec((nseq, tb, c), lambda b, t: (b, t, 0))
    ps = [p.reshape(bsz, seq, p.shape[-1]) for p in (p_rw, p_lru, p_gla)]
    consts = tuple(rw_params) + tuple(lru_params) + tuple(gla_params)
    widths = (RW_WIDTH, LRU_WIDTH, GLA_WIDTH)
    outs = pl.pallas_call(
        functools.partial(_mixers_kernel, tb=tb, nseq=nseq),
        grid=(bsz // nseq, nt),
        in_specs=[blk(p.shape[-1]) for p in ps] + [_const_spec(q.shape) for q in consts],
        out_specs=[blk(w) for w in widths],
        out_shape=[jax.ShapeDtypeStruct((bsz, seq, w), F32) for w in widths],
        scratch_shapes=[pltpu.VMEM((nseq, SUBLANES, p_rw.shape[-1]), F32),
                        pltpu.VMEM((nseq, 2, LANES, LANES), F32),
                        pltpu.VMEM((nseq, SUBLANES, LRU_WIDTH), F32),
                        pltpu.VMEM((nseq, 1, LRU_WIDTH), F32),
                        pltpu.VMEM((nseq, GLA_WIDTH, GLA_KEY_WIDTH), F32)],
        compiler_params=pltpu.CompilerParams(dimension_semantics=("arbitrary", "arbitrary"),
                                             vmem_limit_bytes=VMEM_LIMIT),
        name="mixers",
    )(*ps, *consts)
    return [y.reshape(bsz * seq, y.shape[-1]) for y in outs]


def _post_kernel(yrw_ref, ylru_ref, ygla_ref, x_ref, wout_ref, g2_ref, wg_ref, wu_ref, wd_ref,
                 gf_ref, o_ref, *, ff_chunk, final):
    y = jnp.concatenate([yrw_ref[...], ylru_ref[...], ygla_ref[...]], axis=1).astype(BF16)
    x = x_ref[...] + jnp.dot(y, wout_ref[...], preferred_element_type=F32)
    hn = _rms_rows(x, g2_ref[...]).astype(BF16)
    d_ff = wg_ref.shape[1]
    acc = x
    for c in range(d_ff // ff_chunk):
        cols = slice(c * ff_chunk, (c + 1) * ff_chunk)
        gate = jnp.dot(hn, wg_ref[:, cols], preferred_element_type=F32)
        up = jnp.dot(hn, wu_ref[:, cols], preferred_element_type=F32)
        h = (gate * _sigmoid(gate) * up).astype(BF16)
        acc = acc + jnp.dot(h, wd_ref[cols, :], preferred_element_type=F32)
    if final:
        acc = _rms_rows(acc, gf_ref[...])
    o_ref[...] = acc


def _post(y_rw, y_lru, y_gla, xf, w_out, g2, w_gate, w_up, w_down, gf, tm, final):
    n, d = xf.shape
    row = lambda c: pl.BlockSpec((tm, c), lambda i: (i, 0))
    consts = (w_out, g2, w_gate, w_up, w_down, gf)
    return pl.pallas_call(
        functools.partial(_post_kernel, ff_chunk=256, final=final),
        grid=(n // tm,),
        in_specs=[row(y_rw.shape[1]), row(y_lru.shape[1]), row(y_gla.shape[1]), row(d)]
        + [_const_spec(q.shape) for q in consts],
        out_specs=row(d),
        out_shape=jax.ShapeDtypeStruct((n, d), F32),
        compiler_params=pltpu.CompilerParams(dimension_semantics=("arbitrary",),
                                             vmem_limit_bytes=VMEM_LIMIT),
        name="post",
    )(y_rw, y_lru, y_gla, xf, *consts)


def _row(v):
    return v.reshape(1, -1).astype(F32)


def _block_diag_groups(w, per_group):
    nb, c, _ = w.shape
    groups = nb // per_group
    w = w.reshape(groups, per_group, c, c)
    eye = jnp.eye(per_group, dtype=w.dtype)
    out = jnp.einsum("gbij,bk->gbikj", w, eye)
    return out.reshape(groups, per_group * c, per_group * c)


def kernel(x, norm1_g, w_in, rw_mu, rw_w0, rw_w_up, rw_a0, rw_a_up, rw_g_up, rw_k_k, rw_k_a, rw_r_k, rw_ln_g, rw_ln_b, lru_conv_w, lru_conv_b, lru_wa, lru_ba, lru_wx, lru_bx, lru_lam, lru_norm_g, gla_gk_up, gla_gk_b, gla_norm_g, w_out, norm2_g, ffn_w_gate, ffn_w_up, ffn_w_down, final_norm_g):
    bsz, seq, d = x.shape
    depth = w_in.shape[0]
    n = bsz * seq
    tm = 512 if n % 512 == 0 else 256
    tb = 256
    nseq = 2 if bsz % 2 == 0 else 1
    rw_proj = rw_mu.shape[1]
    lru_proj = 2 * LRU_WIDTH
    xf = x.reshape(n, d)
    for l in range(depth):
        w_l = w_in[l]
        w_rw = w_l[:, :rw_proj].astype(BF16)
        w_lru = w_l[:, rw_proj:rw_proj + lru_proj].astype(BF16)
        w_gla = jnp.pad(w_l[:, rw_proj + lru_proj:], ((0, 0), (0, GLA_PAD - (w_l.shape[1] - rw_proj - lru_proj))))
        gq = 2 * GLA_KEY_WIDTH + GLA_WIDTH
        w_gla = jnp.concatenate([w_gla[:, :gq], w_gla[:, gq + GLA_GATE_RANK:gq + GLA_GATE_RANK + GLA_WIDTH],
                                 w_gla[:, gq:gq + GLA_GATE_RANK],
                                 w_gla[:, gq + GLA_GATE_RANK + GLA_WIDTH:]], axis=1).astype(BF16)
        p_rw, p_lru, p_gla = _inproj(xf, _row(norm1_g[l]), w_rw, w_lru, w_gla, tm)

        wup = jnp.pad(rw_w_up[l], ((0, HEAD), (0, 0))).astype(BF16)
        aup = jnp.pad(rw_a_up[l], ((HEAD, 0), (0, 0))).astype(BF16)
        rw_params = (_row(rw_mu[l]), _row(rw_w0[l]), wup, _row(rw_a0[l]), aup, rw_g_up[l].astype(BF16),
                     _row(rw_k_k[l]), _row(rw_k_a[l]), _row(rw_r_k[l]), _row(rw_ln_g[l]), _row(rw_ln_b[l]))
        lru_params = (lru_conv_w[l].astype(F32), _row(lru_conv_b[l]),
                      _block_diag_groups(lru_wa[l], 4).astype(BF16), _row(lru_ba[l]),
                      _block_diag_groups(lru_wx[l], 4).astype(BF16), _row(lru_bx[l]),
                      _row(lru_lam[l]), _row(lru_norm_g[l]))
        gkup = jnp.pad(gla_gk_up[l], ((0, LANES - GLA_GATE_RANK), (0, 0))).astype(F32)
        gla_params = (gkup, _row(gla_gk_b[l]), _row(jnp.tile(gla_norm_g[l], GLA_WIDTH // HEAD)))
        y_rw, y_lru, y_gla = _mixers(p_rw, p_lru, p_gla, rw_params, lru_params, gla_params,
                                     bsz, seq, tb, nseq)

        xf = _post(y_rw, y_lru, y_gla, xf, w_out[l].astype(BF16), _row(norm2_g[l]),
                   ffn_w_gate[l].astype(BF16), ffn_w_up[l].astype(BF16), ffn_w_down[l].astype(BF16),
                   _row(final_norm_g), tm, final=(l == depth - 1))
    return xf.reshape(bsz, seq, d)
```

```python
import functools
import math

import jax
import jax.numpy as jnp
from jax import lax
from jax.experimental import pallas as pl
from jax.experimental.pallas import tpu as pltpu

F32 = jnp.float32
BF16 = jnp.bfloat16

HEAD = 64
NORM_EPS = 1e-6
RW_WIDTH = 256
RW_DECAY_SCALE = math.exp(-0.5)
RW_LN_EPS = 64e-5
LRU_WIDTH = 512
LRU_CONV = 4
LRU_C = 8.0
GLA_WIDTH = 256
GLA_KEY_WIDTH = 128
GLA_DK = 32
GLA_GATE_RANK = 16
GLA_GATE_NORM = 16.0
GLA_PAD = 896

CHUNK = 64
LANES = 128
SUBLANES = 8
VMEM_LIMIT = 56 * 1024 * 1024


def _dot(a, b):
    return jnp.dot(a.astype(BF16), b.astype(BF16), preferred_element_type=F32)


def _dot_nt(a, b):
    return lax.dot_general(a.astype(BF16), b.astype(BF16), (((1,), (1,)), ((), ())),
                           preferred_element_type=F32)


def _dot_tn(a, b):
    return lax.dot_general(a.astype(BF16), b.astype(BF16), (((0,), (0,)), ((), ())),
                           preferred_element_type=F32)


def _split(x):
    hi = x.astype(BF16)
    lo = (x - hi.astype(F32)).astype(BF16)
    return hi, lo


def _group_sum(x, ones):
    return jnp.dot(x.astype(BF16), ones, preferred_element_type=F32)


def _chunk_cumsums(tril_c, ones_c, x):
    hi, lo = _split(x)
    cs = jnp.dot(tril_c, hi, preferred_element_type=F32) + jnp.dot(tril_c, lo, preferred_element_type=F32)
    cl = jnp.dot(ones_c, hi, preferred_element_type=F32) + jnp.dot(ones_c, lo, preferred_element_type=F32)
    return cs, cl


def _dot3(a, b):
    ah, al = _split(a)
    bh, bl = _split(b)
    return (jnp.dot(ah, bh, preferred_element_type=F32)
            + jnp.dot(al, bh, preferred_element_type=F32)
            + jnp.dot(ah, bl, preferred_element_type=F32))


def _iota(shape, dim):
    return lax.broadcasted_iota(jnp.int32, shape, dim)


def _group_ones(n, group):
    r = _iota((n, n), 0) // group
    c = _iota((n, n), 1) // group
    return jnp.where(r == c, 1.0, 0.0).astype(BF16)


def _chunk_sum_ones(tb):
    tr = _iota((tb, tb), 0)
    tc = _iota((tb, tb), 1)
    same = (tr // CHUNK) == (tc // CHUNK)
    return (jnp.where(same & (tr >= tc), 1.0, 0.0).astype(BF16),
            jnp.where(same, 1.0, 0.0).astype(BF16))


def _rms_rows(x, g):
    ms = jnp.mean(x * x, axis=-1, keepdims=True)
    return x * lax.rsqrt(ms + NORM_EPS) * g


def _shift_rows(x, prev_tail, j):
    xs = pltpu.roll(x, j, 0)
    fix = pltpu.roll(prev_tail, j, 0)
    row = _iota((SUBLANES, x.shape[1]), 0)
    head = jnp.where(row < j, fix, xs[0:SUBLANES])
    return jnp.concatenate([head, xs[SUBLANES:]], axis=0)


def _sigmoid(x):
    return 1.0 / (1.0 + jnp.exp2(x * (-math.log2(math.e))))


def _sqrt_nonneg(y):
    return jnp.where(y > 0.0, y * lax.rsqrt(y), 0.0)


def _const_spec(shape):
    nd = len(shape)
    return pl.BlockSpec(shape, lambda *_: (0,) * nd)


def _layer_spec(arr, layer):
    nd = arr.ndim - 1
    return pl.BlockSpec((None,) + arr.shape[1:], lambda *_: (layer,) + (0,) * nd)


def _inproj_kernel(x_ref, g_ref, w_ref, prw_ref, plru_ref, pgla_ref, *, tm, tb):
    hn = _rms_rows(x_ref[...], g_ref[...]).astype(BF16)
    rw_w = prw_ref.shape[1]
    lru_w = plru_ref.shape[0] * LANES
    prw_ref[...] = jnp.dot(hn, w_ref[:, 0:rw_w], preferred_element_type=F32)
    p_lru = jnp.dot(hn, w_ref[:, rw_w:rw_w + lru_w], preferred_element_type=F32)
    seg = tb // SUBLANES
    for blk in range(tm // tb):
        for s in range(SUBLANES):
            rows = slice(blk * tb + s * seg, blk * tb + (s + 1) * seg)
            for c in range(lru_w // LANES):
                plru_ref[c, pl.ds(blk * tb + s, seg, stride=SUBLANES), :] = (
                    p_lru[rows, c * LANES:(c + 1) * LANES])
    pgla_ref[...] = jnp.dot(hn, w_ref[:, rw_w + lru_w:], preferred_element_type=F32)


def _inproj(xf, g, w, layer, tm, tb):
    n, d = xf.shape
    rw_w, lru_w = 1024, 2 * LRU_WIDTH
    gla_w = w.shape[2] - rw_w - lru_w
    row = lambda c: pl.BlockSpec((tm, c), lambda i: (i, 0))
    return pl.pallas_call(
        functools.partial(_inproj_kernel, tm=tm, tb=tb),
        grid=(n // tm,),
        in_specs=[row(d), _layer_spec(g, layer), _layer_spec(w, layer)],
        out_specs=[row(rw_w), pl.BlockSpec((lru_w // LANES, tm, LANES), lambda i: (0, i, 0)), row(gla_w)],
        out_shape=[jax.ShapeDtypeStruct((n, rw_w), F32),
                   jax.ShapeDtypeStruct((lru_w // LANES, n, LANES), F32),
                   jax.ShapeDtypeStruct((n, gla_w), F32)],
        compiler_params=pltpu.CompilerParams(dimension_semantics=("arbitrary",),
                                             vmem_limit_bytes=VMEM_LIMIT),
        name="inproj",
    )(xf, g, w)


def _rwkv_steps(p_ref, prm, y_ref, tail_ref, s_ref, tb):
    (mu_ref, w0_ref, wup_ref, a0_ref, aup_ref, gup_ref, kk_ref, ka_ref, rk_ref, lng_ref,
     lnb_ref) = prm
    p = p_ref[...]
    shifted = _shift_rows(p, tail_ref[...], 1)
    tail_ref[...] = p[tb - SUBLANES:tb, :]
    ps = p + (shifted - p) * mu_ref[...]
    r = ps[:, 0:256]
    k = ps[:, 256:512]
    v = ps[:, 512:768]
    lo_wa = ps[:, 768:896]
    g_lo = ps[:, 896:1024]
    ones_h = _group_ones(RW_WIDTH, HEAD)
    kk = k * kk_ref[...]
    w_raw = w0_ref[...] + _dot(jnp.tanh(lo_wa), wup_ref[...])
    a = _sigmoid(a0_ref[...] + _dot(lo_wa, aup_ref[...]))
    g = _dot(_sigmoid(g_lo), gup_ref[...])
    kk_sq = _group_sum(kk * kk, ones_h)
    yield
    lw = -RW_DECAY_SCALE * _sigmoid(w_raw)
    kk = kk * lax.rsqrt(jnp.maximum(kk_sq, 1e-24))
    k = k * (1.0 + (a - 1.0) * ka_ref[...])
    a_neg = -kk
    b = kk * a

    tril_c, ones_c = _chunk_sum_ones(tb)
    cs, cl = _chunk_cumsums(tril_c, ones_c, lw)
    yield
    inv = jnp.exp(-cs)
    dec = jnp.exp(cl - cs)
    wl = jnp.exp(cl)
    at, rt, bt, kt = a_neg * jnp.exp(cs - lw), r * jnp.exp(cs), b * inv, k * inv
    bd, kd = b * dec, k * dec

    rr = _iota((LANES, LANES), 0)
    cc = _iota((LANES, LANES), 1)
    same = (rr // CHUNK) == (cc // CHUNK)
    strict = same & (cc < rr)
    incl = same & (cc <= rr)
    eye = jnp.where(rr == cc, 1.0, 0.0).astype(F32)
    m0 = _iota((CHUNK, LANES), 1) < HEAD
    zeros = jnp.zeros((LANES, LANES), F32)

    def stack(x):
        return jnp.concatenate([jnp.where(m0, x, 0.0), jnp.where(m0, 0.0, x)], axis=0)

    def tile(x):
        return jnp.concatenate([x, x], axis=0)

    nc = tb // CHUNK
    idx = [(c, pair) for c in range(nc) for pair in range(2)]

    def blk(x, c, pair):
        return x[c * CHUNK:(c + 1) * CHUNK, pair * LANES:(pair + 1) * LANES]

    a_st = [stack(blk(at, *i)) for i in idx]
    ar = [jnp.concatenate([a_st[j], stack(blk(rt, *i))], axis=0) for j, i in enumerate(idx)]
    gm = [_dot_nt(ar[j], jnp.concatenate([tile(blk(bt, *i)), tile(blk(kt, *i))], axis=0))
          for j, i in enumerate(idx)]
    yield
    n_mat = [jnp.where(strict, m[0:LANES, 0:LANES], 0.0) for m in gm]
    l_ak = [jnp.where(strict, m[0:LANES, LANES:], 0.0) for m in gm]
    l_rbk = [jnp.concatenate([jnp.where(incl, m[LANES:, 0:LANES], 0.0),
                              jnp.where(incl, m[LANES:, LANES:], 0.0)], axis=1) for m in gm]
    v_bd = [stack(blk(v, *i)) for i in idx]
    lv = [_dot(l, vb) for l, vb in zip(l_ak, v_bd)]

    d4 = (rr // 4) == (cc // 4)
    nd = [jnp.where(d4, m, 0.0) for m in n_mat]
    nd2 = [_dot(m, m) for m in nd]
    yield
    t_inv = [eye + m + _dot(m2, eye + m) for m, m2 in zip(nd, nd2)]
    yield
    s = 4
    while s < CHUNK:
        off = ((rr // (2 * s)) == (cc // (2 * s))) & ((rr // s) != (cc // s))
        nt = [_dot(jnp.where(off, m, 0.0), t) for m, t in zip(n_mat, t_inv)]
        yield
        t_inv = [t + _dot(t, m) for t, m in zip(t_inv, nt)]
        yield
        s *= 2

    x = [_dot(t, jnp.concatenate([a_, l], axis=1)) for t, a_, l in zip(t_inv, a_st, lv)]
    yield
    z = [jnp.concatenate([xx, jnp.concatenate([zeros, vb], axis=1)], axis=0) for xx, vb in zip(x, v_bd)]
    y2 = [_dot(l, zz) for l, zz in zip(l_rbk, z)]
    ms = [_dot_tn(zz, jnp.concatenate([stack(blk(bd, *i)), stack(blk(kd, *i))], axis=0))
          for zz, i in zip(z, idx)]
    yield

    outs = []
    states = [s_ref[0], s_ref[1]]
    for c in range(nc):
        halves = []
        for pair in range(2):
            j = 2 * c + pair
            state = states[pair]
            y_bd = _dot_nt(ar[j][LANES:] + y2[j][:, 0:LANES], state) + y2[j][:, LANES:]
            halves.append(y_bd[0:CHUNK] + y_bd[CHUNK:])
            states[pair] = (state * blk(wl, c, pair)[0:1] + _dot(state, ms[j][0:LANES])
                            + ms[j][LANES:])
        outs.append(jnp.concatenate(halves, axis=1))
        yield
    s_ref[0] = states[0]
    s_ref[1] = states[1]

    y = jnp.concatenate(outs, axis=0)
    mean = _group_sum(y, ones_h) * (1.0 / HEAD)
    bonus_s = _group_sum(r * k * rk_ref[...], ones_h)
    yield
    d = y - mean
    var = _group_sum(d * d, ones_h) * (1.0 / HEAD)
    yield
    yn = d * lax.rsqrt(var + RW_LN_EPS) * lng_ref[...] + lnb_ref[...]
    y_ref[...] = (yn + bonus_s * v) * g


def _lru_steps(slab, prm, y_ref, tail_ref, h_ref, unperm_ref, tb):
    cw_ref, cb_ref, wa_ref, ba_ref, wx_ref, bx_ref, lam_ref, ng_ref = prm
    nslab = LRU_WIDTH // LANES
    seg = tb // SUBLANES
    xb = jnp.concatenate([slab(c) for c in range(nslab)], axis=1)
    gate = jnp.concatenate([slab(nslab + c) for c in range(nslab)], axis=1)
    sub0 = _iota((SUBLANES, LRU_WIDTH), 0) == 0

    def group(x, g):
        return x[g * SUBLANES:(g + 1) * SUBLANES]

    wrapped = []
    for g in range(seg - (LRU_CONV - 1), seg):
        prev = tail_ref[(g - seg + LRU_CONV - 1) * SUBLANES:(g - seg + LRU_CONV) * SUBLANES, :]
        wrapped.append(jnp.where(sub0, pltpu.roll(prev, 1, 0), pltpu.roll(group(xb, g), 1, 0)))
    tail_ref[...] = xb[(seg - (LRU_CONV - 1)) * SUBLANES:, :]
    xc = cb_ref[...] + xb * cw_ref[LRU_CONV - 1:LRU_CONV, :]
    for j in range(1, LRU_CONV):
        shifted = jnp.concatenate(wrapped[LRU_CONV - 1 - j:] + [xb[0:(seg - j) * SUBLANES]], axis=0)
        xc = xc + shifted * cw_ref[LRU_CONV - 1 - j:LRU_CONV - j, :]

    half = LRU_WIDTH // 2
    z_r = jnp.concatenate([_dot(xc[:, 0:half], wa_ref[0]), _dot(xc[:, half:], wa_ref[1])], axis=1)
    z_i = jnp.concatenate([_dot(xc[:, 0:half], wx_ref[0]), _dot(xc[:, half:], wx_ref[1])], axis=1)
    yield
    gate_r = _sigmoid(z_r + ba_ref[...])
    gate_i = _sigmoid(z_i + bx_ref[...])
    lam = lam_ref[...]
    softplus_neg_lam = jnp.maximum(-lam, 0.0) + jnp.log1p(jnp.exp(-jnp.abs(lam)))
    log_a = (-LRU_C * gate_r) * softplus_neg_lam
    a = jnp.exp(log_a)
    u = _sqrt_nonneg(1.0 - a * a) * (gate_i * xc)

    h0 = [group(u, 0)]
    ap = [group(a, 0)]
    for g in range(1, seg):
        h0.append(group(a, g) * h0[-1] + group(u, g))
        ap.append(group(a, g) * ap[-1])
    h_end, a_end = h0[-1], ap[-1]
    row = _iota((SUBLANES, LRU_WIDTH), 0)
    for d in (1, 2, 4):
        keep = row >= d
        h_end = a_end * jnp.where(keep, pltpu.roll(h_end, d, 0), 0.0) + h_end
        a_end = a_end * jnp.where(keep, pltpu.roll(a_end, d, 0), 1.0)
    h_true = h_end + a_end * h_ref[...]
    h_in = jnp.where(sub0, h_ref[...], pltpu.roll(h_true, 1, 0))
    h_ref[...] = h_true[SUBLANES - 1:SUBLANES, :]
    h = jnp.concatenate([h0[g] + ap[g] * h_in for g in range(seg)], axis=0)

    c2 = 2.0 * math.sqrt(2.0 / math.pi)
    y = h * (gate * _sigmoid(gate * (c2 + (c2 * 0.044715) * (gate * gate))))
    ones_h = _group_ones(half, HEAD)
    ysq = y * y
    ms = jnp.concatenate([_group_sum(ysq[:, 0:half], ones_h),
                          _group_sum(ysq[:, half:], ones_h)], axis=1) * (1.0 / HEAD)
    yield
    y = y * lax.rsqrt(ms + NORM_EPS) * ng_ref[...]
    for c in range(nslab):
        unperm_ref[c] = y[:, c * LANES:(c + 1) * LANES]
    for s in range(SUBLANES):
        y_ref[s * seg:(s + 1) * seg, :] = jnp.concatenate(
            [unperm_ref[c, pl.ds(s, seg, stride=SUBLANES), :] for c in range(nslab)], axis=1)


def _gla_steps(p_ref, prm, y_ref, s_ref, tb):
    gkup_ref, gkb_ref, ng_ref = prm
    kw = GLA_KEY_WIDTH
    q = p_ref[:, 0:kw] * (GLA_DK ** -0.5)
    k = p_ref[:, kw:2 * kw]
    v = p_ref[:, 2 * kw:2 * kw + GLA_WIDTH]
    x = _dot3(p_ref[:, 768:GLA_PAD], gkup_ref[...]) + gkb_ref[...]
    yield
    la = (jnp.minimum(x, 0.0) - jnp.log1p(jnp.exp(-jnp.abs(x)))) * (1.0 / GLA_GATE_NORM)

    tril_c, ones_c = _chunk_sum_ones(tb)
    bc, bl = _chunk_cumsums(tril_c, ones_c, la)
    yield
    qi = q * jnp.exp(bc)
    ki = k * jnp.exp(-bc)
    kd = k * jnp.exp(bl - bc)
    dec = jnp.exp(bl)

    nh = GLA_WIDTH // HEAD
    lane_k = _iota((tb, kw), 1) // GLA_DK
    lane_v = _iota((tb, GLA_WIDTH), 1) // HEAD
    qs = jnp.concatenate([jnp.where(lane_k == h, qi, 0.0) for h in range(nh)], axis=0)
    st = _iota((nh * tb, tb), 0) % tb
    sj = _iota((nh * tb, tb), 1)
    causal = ((st // CHUNK) == (sj // CHUNK)) & (st >= sj)
    sc = _dot_nt(qs, ki)
    diag = (_iota((GLA_WIDTH, kw), 0) // HEAD) == (_iota((GLA_WIDTH, kw), 1) // GLA_DK)
    nc = tb // CHUNK
    upd = [jnp.where(diag, _dot_tn(v[c * CHUNK:(c + 1) * CHUNK], kd[c * CHUNK:(c + 1) * CHUNK]), 0.0)
           for c in range(nc)]
    yield
    o_full = _dot(jnp.where(causal, sc, 0.0), v)
    yield
    o_intra = jnp.where(lane_v == 0, o_full[0:tb], 0.0)
    for h in range(1, nh):
        o_intra = o_intra + jnp.where(lane_v == h, o_full[h * tb:(h + 1) * tb], 0.0)

    state = s_ref[...]
    outs = []
    for c in range(nc):
        rows = slice(c * CHUNK, (c + 1) * CHUNK)
        outs.append(o_intra[rows] + _dot_nt(qi[rows], state))
        state = state * dec[c * CHUNK:c * CHUNK + 1, :] + upd[c]
    s_ref[...] = state

    o = jnp.concatenate(outs, axis=0)
    ms = _group_sum(o * o, _group_ones(GLA_WIDTH, HEAD)) * (1.0 / HEAD)
    yield
    gate = p_ref[:, 512:768]
    y_ref[...] = o * lax.rsqrt(ms + NORM_EPS) * ng_ref[...] * (gate * _sigmoid(gate))


N_RW_PARAMS = 11
N_LRU_PARAMS = 8
N_GLA_PARAMS = 3


def _mixers_kernel(*refs, tb, nseq):
    prw_ref, plru_ref, pgla_ref = refs[0:3]
    o = 3
    rw_prm = refs[o:o + N_RW_PARAMS]
    o += N_RW_PARAMS
    lru_prm = refs[o:o + N_LRU_PARAMS]
    o += N_LRU_PARAMS
    gla_prm = refs[o:o + N_GLA_PARAMS]
    o += N_GLA_PARAMS
    yrw_ref, ylru_ref, ygla_ref = refs[o:o + 3]
    rw_tail, rw_state, lru_tail, lru_h, lru_unperm, gla_state = refs[o + 3:]

    @pl.when(pl.program_id(1) == 0)
    def _():
        for ref in (rw_tail, rw_state, lru_tail, lru_h, gla_state):
            ref[...] = jnp.zeros_like(ref)

    gens = []
    for q in range(nseq):
        gens.append(_rwkv_steps(prw_ref.at[q], rw_prm, yrw_ref.at[q], rw_tail.at[q], rw_state.at[q], tb))
        gens.append(_gla_steps(pgla_ref.at[q], gla_prm, ygla_ref.at[q], gla_state.at[q], tb))
        gens.append(_lru_steps(lambda c, q=q: plru_ref[c, q], lru_prm, ylru_ref.at[q], lru_tail.at[q],
                               lru_h.at[q], lru_unperm.at[q], tb))
    while gens:
        alive = []
        for gen in gens:
            try:
                next(gen)
                alive.append(gen)
            except StopIteration:
                pass
        gens = alive


def _mixers(p_rw, p_lru, p_gla, params, layer, bsz, seq, tb, nseq):
    nt = seq // tb
    blk = lambda c: pl.BlockSpec((nseq, tb, c), lambda b, t: (b, t, 0))
    nslab = p_lru.shape[0]
    p_rw = p_rw.reshape(bsz, seq, p_rw.shape[-1])
    p_gla = p_gla.reshape(bsz, seq, p_gla.shape[-1])
    p_lru = p_lru.reshape(nslab, bsz, seq, LANES)
    widths = (RW_WIDTH, LRU_WIDTH, GLA_WIDTH)
    outs = pl.pallas_call(
        functools.partial(_mixers_kernel, tb=tb, nseq=nseq),
        grid=(bsz // nseq, nt),
        in_specs=[blk(p_rw.shape[-1]),
                  pl.BlockSpec((nslab, nseq, tb, LANES), lambda b, t: (0, b, t, 0)),
                  blk(p_gla.shape[-1])] + [_layer_spec(q, layer) for q in params],
        out_specs=[blk(w) for w in widths],
        out_shape=[jax.ShapeDtypeStruct((bsz, seq, w), F32) for w in widths],
        scratch_shapes=[pltpu.VMEM((nseq, SUBLANES, p_rw.shape[-1]), F32),
                        pltpu.VMEM((nseq, 2, LANES, LANES), F32),
                        pltpu.VMEM((nseq, (LRU_CONV - 1) * SUBLANES, LRU_WIDTH), F32),
                        pltpu.VMEM((nseq, 1, LRU_WIDTH), F32),
                        pltpu.VMEM((nseq, LRU_WIDTH // LANES, tb, LANES), F32),
                        pltpu.VMEM((nseq, GLA_WIDTH, GLA_KEY_WIDTH), F32)],
        compiler_params=pltpu.CompilerParams(dimension_semantics=("arbitrary", "arbitrary"),
                                             vmem_limit_bytes=VMEM_LIMIT),
        name="mixers",
    )(p_rw, p_lru, p_gla, *params)
    return [y.reshape(bsz * seq, y.shape[-1]) for y in outs]


def _post_kernel(yrw_ref, ylru_ref, ygla_ref, x_ref, wout_ref, g2_ref, wg_ref, wu_ref, wd_ref,
                 gf_ref, o_ref, *, ff_chunk, final):
    y = jnp.concatenate([yrw_ref[...], ylru_ref[...], ygla_ref[...]], axis=1).astype(BF16)
    x = x_ref[...] + jnp.dot(y, wout_ref[...], preferred_element_type=F32)
    hn = _rms_rows(x, g2_ref[...]).astype(BF16)
    d_ff = wg_ref.shape[1]
    acc = x
    for c in range(d_ff // ff_chunk):
        cols = slice(c * ff_chunk, (c + 1) * ff_chunk)
        gate = jnp.dot(hn, wg_ref[:, cols], preferred_element_type=F32)
        up = jnp.dot(hn, wu_ref[:, cols], preferred_element_type=F32)
        h = (gate * _sigmoid(gate) * up).astype(BF16)
        acc = acc + jnp.dot(h, wd_ref[cols, :], preferred_element_type=F32)
    if final:
        acc = _rms_rows(acc, gf_ref[...])
    o_ref[...] = acc


def _post(y_rw, y_lru, y_gla, xf, w_out, g2, w_gate, w_up, w_down, gf, layer, tm, final):
    n, d = xf.shape
    row = lambda c: pl.BlockSpec((tm, c), lambda i: (i, 0))
    stacked = (w_out, g2, w_gate, w_up, w_down)
    return pl.pallas_call(
        functools.partial(_post_kernel, ff_chunk=256, final=final),
        grid=(n // tm,),
        in_specs=[row(y_rw.shape[1]), row(y_lru.shape[1]), row(y_gla.shape[1]), row(d)]
        + [_layer_spec(q, layer) for q in stacked] + [_const_spec(gf.shape)],
        out_specs=row(d),
        out_shape=jax.ShapeDtypeStruct((n, d), F32),
        compiler_params=pltpu.CompilerParams(dimension_semantics=("arbitrary",),
                                             vmem_limit_bytes=VMEM_LIMIT),
        name="post",
    )(y_rw, y_lru, y_gla, xf, *stacked, gf)


def _rows(v):
    return v.reshape(v.shape[0], 1, -1).astype(F32)


def _block_diag_groups(w, per_group):
    nl, nb, c, _ = w.shape
    groups = nb // per_group
    w = w.reshape(nl, groups, per_group, c, c)
    eye = jnp.eye(per_group, dtype=w.dtype)
    out = jnp.einsum("lgbij,bk->lgbikj", w, eye)
    return out.reshape(nl, groups, per_group * c, per_group * c)


def kernel(x, norm1_g, w_in, rw_mu, rw_w0, rw_w_up, rw_a0, rw_a_up, rw_g_up, rw_k_k, rw_k_a, rw_r_k, rw_ln_g, rw_ln_b, lru_conv_w, lru_conv_b, lru_wa, lru_ba, lru_wx, lru_bx, lru_lam, lru_norm_g, gla_gk_up, gla_gk_b, gla_norm_g, w_out, norm2_g, ffn_w_gate, ffn_w_up, ffn_w_down, final_norm_g):
    bsz, seq, d = x.shape
    depth = w_in.shape[0]
    n = bsz * seq
    tb = 256
    tm = 512 if n % 512 == 0 else tb
    nseq = 2 if bsz % 2 == 0 else 1

    gq = 1024 + 2 * LRU_WIDTH + 2 * GLA_KEY_WIDTH + GLA_WIDTH
    w_in_b = jnp.concatenate(
        [w_in[..., :gq], w_in[..., gq + GLA_GATE_RANK:], w_in[..., gq:gq + GLA_GATE_RANK],
         jnp.zeros(w_in.shape[:2] + (GLA_PAD - (w_in.shape[2] - 1024 - 2 * LRU_WIDTH),), w_in.dtype)],
        axis=-1).astype(BF16)
    pad0 = ((0, 0),)
    mixer_params = (
        _rows(rw_mu), _rows(rw_w0),
        jnp.pad(rw_w_up, pad0 + ((0, HEAD), (0, 0))).astype(BF16),
        _rows(rw_a0),
        jnp.pad(rw_a_up, pad0 + ((HEAD, 0), (0, 0))).astype(BF16),
        rw_g_up.astype(BF16), _rows(rw_k_k), _rows(rw_k_a), _rows(rw_r_k), _rows(rw_ln_g), _rows(rw_ln_b),
        lru_conv_w.astype(F32), _rows(lru_conv_b),
        _block_diag_groups(lru_wa, 4).astype(BF16), _rows(lru_ba),
        _block_diag_groups(lru_wx, 4).astype(BF16), _rows(lru_bx), _rows(lru_lam), _rows(lru_norm_g),
        jnp.pad(gla_gk_up, pad0 + ((0, LANES - GLA_GATE_RANK), (0, 0))).astype(F32), _rows(gla_gk_b),
        _rows(jnp.tile(gla_norm_g, (1, GLA_WIDTH // HEAD))))
    g1, g2 = _rows(norm1_g), _rows(norm2_g)
    w_out_b, w_gate_b, w_up_b, w_down_b = (w.astype(BF16) for w in (w_out, ffn_w_gate, ffn_w_up, ffn_w_down))
    gf = final_norm_g.reshape(1, -1).astype(F32)

    xf = x.reshape(n, d)
    for l in range(depth):
        p_rw, p_lru, p_gla = _inproj(xf, g1, w_in_b, l, tm, tb)
        y_rw, y_lru, y_gla = _mixers(p_rw, p_lru, p_gla, mixer_params, l, bsz, seq, tb, nseq)
        xf = _post(y_rw, y_lru, y_gla, xf, w_out_b, g2, w_gate_b, w_up_b, w_down_b, gf, l, tm,
                   final=(l == depth - 1))
    return xf.reshape(bsz, seq, d)
```

```python
import functools
import math

import jax
import jax.numpy as jnp
from jax import lax
from jax.experimental import pallas as pl
from jax.experimental.pallas import tpu as pltpu

F32 = jnp.float32
BF16 = jnp.bfloat16

HEAD = 64
NORM_EPS = 1e-6
RW_WIDTH = 256
RW_DECAY_SCALE = math.exp(-0.5)
RW_LN_EPS = 64e-5
LRU_WIDTH = 512
LRU_CONV = 4
LRU_C = 8.0
GLA_WIDTH = 256
GLA_KEY_WIDTH = 128
GLA_DK = 32
GLA_GATE_RANK = 16
GLA_GATE_NORM = 16.0
GLA_PAD = 896

CHUNK = 64
LANES = 128
SUBLANES = 8
VMEM_LIMIT = 56 * 1024 * 1024


def _dot(a, b):
    return jnp.dot(a.astype(BF16), b.astype(BF16), preferred_element_type=F32)


def _dot_nt(a, b):
    return lax.dot_general(a.astype(BF16), b.astype(BF16), (((1,), (1,)), ((), ())),
                           preferred_element_type=F32)


def _dot_tn(a, b):
    return lax.dot_general(a.astype(BF16), b.astype(BF16), (((0,), (0,)), ((), ())),
                           preferred_element_type=F32)


def _split(x):
    hi = x.astype(BF16)
    lo = (x - hi.astype(F32)).astype(BF16)
    return hi, lo


def _group_sum(x, ones):
    return jnp.dot(x.astype(BF16), ones, preferred_element_type=F32)


def _chunk_cumsums(tril_c, x):
    hi, lo = _split(x)
    cs = jnp.dot(tril_c, hi, preferred_element_type=F32) + jnp.dot(tril_c, lo, preferred_element_type=F32)
    nc = x.shape[0] // CHUNK
    cl = jnp.concatenate(
        [jnp.broadcast_to(cs[(c + 1) * CHUNK - 1:(c + 1) * CHUNK], (CHUNK, x.shape[1])) for c in range(nc)],
        axis=0)
    return cs, cl


def _dot3(a, b):
    ah, al = _split(a)
    bh, bl = _split(b)
    return (jnp.dot(ah, bh, preferred_element_type=F32)
            + jnp.dot(al, bh, preferred_element_type=F32)
            + jnp.dot(ah, bl, preferred_element_type=F32))


def _iota(shape, dim):
    return lax.broadcasted_iota(jnp.int32, shape, dim)


def _group_ones(n, group):
    r = _iota((n, n), 0) // group
    c = _iota((n, n), 1) // group
    return jnp.where(r == c, 1.0, 0.0).astype(BF16)


def _chunk_tril_ones(tb):
    tr = _iota((tb, tb), 0)
    tc = _iota((tb, tb), 1)
    return jnp.where(((tr // CHUNK) == (tc // CHUNK)) & (tr >= tc), 1.0, 0.0).astype(BF16)


def _rms_rows(x, g):
    ms = jnp.mean(x * x, axis=-1, keepdims=True)
    return x * lax.rsqrt(ms + NORM_EPS) * g


def _shift_rows(x, prev_tail, j):
    xs = pltpu.roll(x, j, 0)
    fix = pltpu.roll(prev_tail, j, 0)
    row = _iota((SUBLANES, x.shape[1]), 0)
    head = jnp.where(row < j, fix, xs[0:SUBLANES])
    return jnp.concatenate([head, xs[SUBLANES:]], axis=0)


def _sigmoid(x):
    return 1.0 / (1.0 + jnp.exp2(x * (-math.log2(math.e))))


def _sqrt_nonneg(y):
    return jnp.where(y > 0.0, y * lax.rsqrt(y), 0.0)


def _const_spec(shape):
    nd = len(shape)
    return pl.BlockSpec(shape, lambda *_: (0,) * nd)


def _layer_spec(arr, layer):
    nd = arr.ndim - 1
    return pl.BlockSpec((None,) + arr.shape[1:], lambda *_: (layer,) + (0,) * nd)


RW_PROJ_W = 1024


def _inproj_kernel(x_ref, g_ref, w_ref, prw_ref, plru_ref, pgla_ref, *, tm, tb):
    hn = _rms_rows(x_ref[...], g_ref[...]).astype(BF16)
    lru_w = plru_ref.shape[0] * LANES
    prw_ref[...] = jnp.dot(hn, w_ref[:, 0:RW_PROJ_W], preferred_element_type=F32)
    p_lru = jnp.dot(hn, w_ref[:, RW_PROJ_W:RW_PROJ_W + lru_w], preferred_element_type=F32)
    seg = tb // SUBLANES
    for blk in range(tm // tb):
        for s in range(SUBLANES):
            rows = slice(blk * tb + s * seg, blk * tb + (s + 1) * seg)
            for c in range(lru_w // LANES):
                plru_ref[c, pl.ds(blk * tb + s, seg, stride=SUBLANES), :] = (
                    p_lru[rows, c * LANES:(c + 1) * LANES])
    gla0 = RW_PROJ_W + lru_w
    qkv_w = 2 * GLA_KEY_WIDTH + GLA_WIDTH
    pgla_ref[:, 0:qkv_w] = jnp.dot(hn, w_ref[:, gla0:gla0 + qkv_w], preferred_element_type=F32)
    rest = jnp.dot(hn, w_ref[:, gla0 + qkv_w:], preferred_element_type=F32)
    pgla_ref[:, qkv_w:qkv_w + GLA_WIDTH] = rest[:, GLA_GATE_RANK:GLA_GATE_RANK + GLA_WIDTH]
    pgla_ref[:, qkv_w + GLA_WIDTH:] = rest[:, 0:LANES]


def _inproj(xf, g, w, layer, tm, tb):
    n, d = xf.shape
    lru_w = 2 * LRU_WIDTH
    gla_w = GLA_PAD
    row = lambda c: pl.BlockSpec((tm, c), lambda i: (i, 0))
    return pl.pallas_call(
        functools.partial(_inproj_kernel, tm=tm, tb=tb),
        grid=(n // tm,),
        in_specs=[row(d), _layer_spec(g, layer), _layer_spec(w, layer)],
        out_specs=[row(RW_PROJ_W), pl.BlockSpec((lru_w // LANES, tm, LANES), lambda i: (0, i, 0)), row(gla_w)],
        out_shape=[jax.ShapeDtypeStruct((n, RW_PROJ_W), F32),
                   jax.ShapeDtypeStruct((lru_w // LANES, n, LANES), F32),
                   jax.ShapeDtypeStruct((n, gla_w), F32)],
        compiler_params=pltpu.CompilerParams(dimension_semantics=("arbitrary",),
                                             vmem_limit_bytes=VMEM_LIMIT),
        name="inproj",
    )(xf, g, w)


def _rwkv_steps(p_ref, prm, y_ref, tail_ref, s_ref, tb):
    (mu_ref, w0_ref, wup_ref, a0_ref, aup_ref, gup_ref, kk_ref, ka_ref, rk_ref, lng_ref,
     lnb_ref) = prm
    p = p_ref[...]
    shifted = _shift_rows(p, tail_ref[...], 1)
    tail_ref[...] = p[tb - SUBLANES:tb, :]
    ps = p + (shifted - p) * mu_ref[...]
    r = ps[:, 0:256]
    k = ps[:, 256:512]
    v = ps[:, 512:768]
    lo_wa = ps[:, 768:896]
    g_lo = ps[:, 896:1024]
    ones_h = _group_ones(RW_WIDTH, HEAD)
    kk = k * kk_ref[...]
    w_raw = w0_ref[...] + _dot(jnp.tanh(lo_wa), wup_ref[...])
    a = _sigmoid(a0_ref[...] + _dot(lo_wa, aup_ref[...]))
    g = _dot(_sigmoid(g_lo), gup_ref[...])
    kk_sq = _group_sum(kk * kk, ones_h)
    yield
    lw = -RW_DECAY_SCALE * _sigmoid(w_raw)
    kk = kk * lax.rsqrt(jnp.maximum(kk_sq, 1e-24))
    k = k * (1.0 + (a - 1.0) * ka_ref[...])
    a_neg = -kk
    b = kk * a

    cs, cl = _chunk_cumsums(_chunk_tril_ones(tb), lw)
    yield
    inv = jnp.exp(-cs)
    dec = jnp.exp(cl - cs)
    wl = jnp.exp(cl)
    at, rt, bt, kt = a_neg * jnp.exp(cs - lw), r * jnp.exp(cs), b * inv, k * inv
    bd, kd = b * dec, k * dec

    rr = _iota((LANES, LANES), 0)
    cc = _iota((LANES, LANES), 1)
    same = (rr // CHUNK) == (cc // CHUNK)
    strict = same & (cc < rr)
    incl = same & (cc <= rr)
    eye = jnp.where(rr == cc, 1.0, 0.0).astype(F32)
    m0 = _iota((CHUNK, LANES), 1) < HEAD
    zeros = jnp.zeros((LANES, LANES), F32)

    def stack(x):
        return jnp.concatenate([jnp.where(m0, x, 0.0), jnp.where(m0, 0.0, x)], axis=0)

    def tile(x):
        return jnp.concatenate([x, x], axis=0)

    nc = tb // CHUNK
    idx = [(c, pair) for c in range(nc) for pair in range(2)]

    def blk(x, c, pair):
        return x[c * CHUNK:(c + 1) * CHUNK, pair * LANES:(pair + 1) * LANES]

    a_st = [stack(blk(at, *i)) for i in idx]
    ar = [jnp.concatenate([a_st[j], stack(blk(rt, *i))], axis=0) for j, i in enumerate(idx)]
    gm = [_dot_nt(ar[j], jnp.concatenate([tile(blk(bt, *i)), tile(blk(kt, *i))], axis=0))
          for j, i in enumerate(idx)]
    yield
    n_mat = [jnp.where(strict, m[0:LANES, 0:LANES], 0.0) for m in gm]
    l_ak = [jnp.where(strict, m[0:LANES, LANES:], 0.0) for m in gm]
    l_rbk = [jnp.concatenate([jnp.where(incl, m[LANES:, 0:LANES], 0.0),
                              jnp.where(incl, m[LANES:, LANES:], 0.0)], axis=1) for m in gm]
    v_bd = [stack(blk(v, *i)) for i in idx]
    lv = [_dot(l, vb) for l, vb in zip(l_ak, v_bd)]

    d4 = (rr // 4) == (cc // 4)
    nd = [jnp.where(d4, m, 0.0) for m in n_mat]
    nd2 = [_dot(m, m) for m in nd]
    yield
    t_inv = [eye + m + _dot(m2, eye + m) for m, m2 in zip(nd, nd2)]
    yield
    s = 4
    while s < CHUNK:
        off = ((rr // (2 * s)) == (cc // (2 * s))) & ((rr // s) != (cc // s))
        if s < SUBLANES:
            nt = [_dot(jnp.where(off, m, 0.0), t) for m, t in zip(n_mat, t_inv)]
            yield
            t_inv = [t + _dot(t, m) for t, m in zip(t_inv, nt)]
            yield
        else:
            def low(m):
                return jnp.concatenate([m[k + s:k + 2 * s] for k in range(0, LANES, 2 * s)], axis=0)

            il = _iota((LANES // 2, LANES), 0)
            rl = (il // s) * (2 * s) + s + il % s
            cl_ = _iota((LANES // 2, LANES), 1)
            off_low = ((rl // (2 * s)) == (cl_ // (2 * s))) & ((rl // s) != (cl_ // s))
            zs = jnp.zeros((s, LANES), F32)
            nt = [_dot(jnp.where(off_low, low(m), 0.0), t) for m, t in zip(n_mat, t_inv)]
            yield
            upd = []
            for t, m in zip(t_inv, nt):
                m_full = jnp.concatenate(
                    [piece for k in range(LANES // (2 * s)) for piece in (zs, m[k * s:(k + 1) * s])], axis=0)
                upd.append(_dot(low(t), m_full))
            yield
            t_inv = [jnp.concatenate(
                [piece for k in range(LANES // (2 * s))
                 for piece in (t[2 * k * s:2 * k * s + s], t[2 * k * s + s:2 * (k + 1) * s] + u[k * s:(k + 1) * s])],
                axis=0) for t, u in zip(t_inv, upd)]
        s *= 2

    x = [_dot(t, jnp.concatenate([a_, l], axis=1)) for t, a_, l in zip(t_inv, a_st, lv)]
    yield
    z = [jnp.concatenate([xx, jnp.concatenate([zeros, vb], axis=1)], axis=0) for xx, vb in zip(x, v_bd)]
    y2 = [_dot(l, zz) for l, zz in zip(l_rbk, z)]
    ms = [_dot_tn(zz, jnp.concatenate([stack(blk(bd, *i)), stack(blk(kd, *i))], axis=0))
          for zz, i in zip(z, idx)]
    yield

    outs = []
    states = [s_ref[0], s_ref[1]]
    for c in range(nc):
        halves = []
        for pair in range(2):
            j = 2 * c + pair
            state = states[pair]
            y_bd = _dot_nt(ar[j][LANES:] + y2[j][:, 0:LANES], state) + y2[j][:, LANES:]
            halves.append(y_bd[0:CHUNK] + y_bd[CHUNK:])
            states[pair] = (state * blk(wl, c, pair)[0:1] + _dot(state, ms[j][0:LANES])
                            + ms[j][LANES:])
        outs.append(jnp.concatenate(halves, axis=1))
        yield
    s_ref[0] = states[0]
    s_ref[1] = states[1]

    y = jnp.concatenate(outs, axis=0)
    mean = _group_sum(y, ones_h) * (1.0 / HEAD)
    bonus_s = _group_sum(r * k * rk_ref[...], ones_h)
    yield
    d = y - mean
    var = _group_sum(d * d, ones_h) * (1.0 / HEAD)
    yield
    yn = d * lax.rsqrt(var + RW_LN_EPS) * lng_ref[...] + lnb_ref[...]
    y_ref[...] = (yn + bonus_s * v) * g


def _lru_steps(slab, prm, y_ref, tail_ref, h_ref, unperm_ref, tb):
    cw_ref, cb_ref, wa_ref, ba_ref, wx_ref, bx_ref, lam_ref, ng_ref = prm
    nslab = LRU_WIDTH // LANES
    seg = tb // SUBLANES
    xb = jnp.concatenate([slab(c) for c in range(nslab)], axis=1)
    gate = jnp.concatenate([slab(nslab + c) for c in range(nslab)], axis=1)
    sub0 = _iota((SUBLANES, LRU_WIDTH), 0) == 0

    def group(x, g):
        return x[g * SUBLANES:(g + 1) * SUBLANES]

    wrapped = []
    for g in range(seg - (LRU_CONV - 1), seg):
        prev = tail_ref[(g - seg + LRU_CONV - 1) * SUBLANES:(g - seg + LRU_CONV) * SUBLANES, :]
        wrapped.append(jnp.where(sub0, pltpu.roll(prev, 1, 0), pltpu.roll(group(xb, g), 1, 0)))
    tail_ref[...] = xb[(seg - (LRU_CONV - 1)) * SUBLANES:, :]
    xc = cb_ref[...] + xb * cw_ref[LRU_CONV - 1:LRU_CONV, :]
    for j in range(1, LRU_CONV):
        shifted = jnp.concatenate(wrapped[LRU_CONV - 1 - j:] + [xb[0:(seg - j) * SUBLANES]], axis=0)
        xc = xc + shifted * cw_ref[LRU_CONV - 1 - j:LRU_CONV - j, :]

    half = LRU_WIDTH // 2
    yield LRU_WAITS[0]
    z_r = jnp.concatenate([_dot(xc[:, 0:half], wa_ref[0]), _dot(xc[:, half:], wa_ref[1])], axis=1)
    z_i = jnp.concatenate([_dot(xc[:, 0:half], wx_ref[0]), _dot(xc[:, half:], wx_ref[1])], axis=1)
    yield
    gate_r = _sigmoid(z_r + ba_ref[...])
    gate_i = _sigmoid(z_i + bx_ref[...])
    lam = lam_ref[...]
    softplus_neg_lam = jnp.maximum(-lam, 0.0) + jnp.log1p(jnp.exp(-jnp.abs(lam)))
    log_a = (-LRU_C * gate_r) * softplus_neg_lam
    a = jnp.exp(log_a)
    u = _sqrt_nonneg(1.0 - a * a) * (gate_i * xc)

    h0 = [group(u, 0)]
    ap = [group(a, 0)]
    for g in range(1, seg):
        h0.append(group(a, g) * h0[-1] + group(u, g))
        ap.append(group(a, g) * ap[-1])
    h_end, a_end = h0[-1], ap[-1]
    row = _iota((SUBLANES, LRU_WIDTH), 0)
    for d in (1, 2, 4):
        keep = row >= d
        h_end = a_end * jnp.where(keep, pltpu.roll(h_end, d, 0), 0.0) + h_end
        a_end = a_end * jnp.where(keep, pltpu.roll(a_end, d, 0), 1.0)
    h_true = h_end + a_end * h_ref[...]
    h_in = jnp.where(sub0, h_ref[...], pltpu.roll(h_true, 1, 0))
    h_ref[...] = h_true[SUBLANES - 1:SUBLANES, :]
    h = jnp.concatenate([h0[g] + ap[g] * h_in for g in range(seg)], axis=0)

    c2 = 2.0 * math.sqrt(2.0 / math.pi)
    y = h * (gate * _sigmoid(gate * (c2 + (c2 * 0.044715) * (gate * gate))))
    ones_h = _group_ones(half, HEAD)
    ysq = y * y
    yield LRU_WAITS[1]
    ms = jnp.concatenate([_group_sum(ysq[:, 0:half], ones_h),
                          _group_sum(ysq[:, half:], ones_h)], axis=1) * (1.0 / HEAD)
    yield
    y = y * lax.rsqrt(ms + NORM_EPS) * ng_ref[...]
    for c in range(nslab):
        unperm_ref[c] = y[:, c * LANES:(c + 1) * LANES]
    for s in range(SUBLANES):
        y_ref[s * seg:(s + 1) * seg, :] = jnp.concatenate(
            [unperm_ref[c, pl.ds(s, seg, stride=SUBLANES), :] for c in range(nslab)], axis=1)


def _gla_steps(p_ref, prm, y_ref, s_ref, tb):
    gkup_ref, gkb_ref, ng_ref = prm
    kw = GLA_KEY_WIDTH
    q = p_ref[:, 0:kw] * (GLA_DK ** -0.5)
    k = p_ref[:, kw:2 * kw]
    v = p_ref[:, 2 * kw:2 * kw + GLA_WIDTH]
    x = _dot3(p_ref[:, 768:GLA_PAD], gkup_ref[...]) + gkb_ref[...]
    yield
    la = (jnp.minimum(x, 0.0) - jnp.log1p(jnp.exp(-jnp.abs(x)))) * (1.0 / GLA_GATE_NORM)

    bc, bl = _chunk_cumsums(_chunk_tril_ones(tb), la)
    yield
    qi = q * jnp.exp(bc)
    ki = k * jnp.exp(-bc)
    kd = k * jnp.exp(bl - bc)
    dec = jnp.exp(bl)

    nh = GLA_WIDTH // HEAD
    lane_k = _iota((tb, kw), 1) // GLA_DK
    lane_v = _iota((tb, GLA_WIDTH), 1) // HEAD
    qs = jnp.concatenate([jnp.where(lane_k == h, qi, 0.0) for h in range(nh)], axis=0)
    st = _iota((nh * tb, tb), 0) % tb
    sj = _iota((nh * tb, tb), 1)
    causal = ((st // CHUNK) == (sj // CHUNK)) & (st >= sj)
    sc = _dot_nt(qs, ki)
    diag = (_iota((GLA_WIDTH, kw), 0) // HEAD) == (_iota((GLA_WIDTH, kw), 1) // GLA_DK)
    nc = tb // CHUNK
    upd = [jnp.where(diag, _dot_tn(v[c * CHUNK:(c + 1) * CHUNK], kd[c * CHUNK:(c + 1) * CHUNK]), 0.0)
           for c in range(nc)]
    yield
    o_full = _dot(jnp.where(causal, sc, 0.0), v)
    yield
    o_intra = jnp.where(lane_v == 0, o_full[0:tb], 0.0)
    for h in range(1, nh):
        o_intra = o_intra + jnp.where(lane_v == h, o_full[h * tb:(h + 1) * tb], 0.0)

    state = s_ref[...]
    outs = []
    for c in range(nc):
        rows = slice(c * CHUNK, (c + 1) * CHUNK)
        outs.append(o_intra[rows] + _dot_nt(qi[rows], state))
        state = state * dec[c * CHUNK:c * CHUNK + 1, :] + upd[c]
    s_ref[...] = state

    o = jnp.concatenate(outs, axis=0)
    ms = _group_sum(o * o, _group_ones(GLA_WIDTH, HEAD)) * (1.0 / HEAD)
    yield
    gate = p_ref[:, 512:768]
    y_ref[...] = o * lax.rsqrt(ms + NORM_EPS) * ng_ref[...] * (gate * _sigmoid(gate))


MIXER_STAGGER = (3, 3)
LRU_WAITS = (2, 6)
N_RW_PARAMS = 11
N_LRU_PARAMS = 8
N_GLA_PARAMS = 3


def _mixers_kernel(*refs, tb, nseq, stagger):
    prw_ref, plru_ref, pgla_ref = refs[0:3]
    o = 3
    rw_prm = refs[o:o + N_RW_PARAMS]
    o += N_RW_PARAMS
    lru_prm = refs[o:o + N_LRU_PARAMS]
    o += N_LRU_PARAMS
    gla_prm = refs[o:o + N_GLA_PARAMS]
    o += N_GLA_PARAMS
    yrw_ref, ylru_ref, ygla_ref = refs[o:o + 3]
    rw_tail, rw_state, lru_tail, lru_h, lru_unperm, gla_state = refs[o + 3:]

    @pl.when(pl.program_id(1) == 0)
    def _():
        for ref in (rw_tail, rw_state, lru_tail, lru_h, gla_state):
            ref[...] = jnp.zeros_like(ref)

    pending = [(0, [_rwkv_steps(prw_ref.at[q], rw_prm, yrw_ref.at[q], rw_tail.at[q], rw_state.at[q], tb)
                    for q in range(nseq)]),
               (stagger[0], [_gla_steps(pgla_ref.at[q], gla_prm, ygla_ref.at[q], gla_state.at[q], tb)
                             for q in range(nseq)]),
               (stagger[1], [_lru_steps(lambda c, q=q: plru_ref[c, q], lru_prm, ylru_ref.at[q],
                                        lru_tail.at[q], lru_h.at[q], lru_unperm.at[q], tb)
                             for q in range(nseq)])]
    pending.sort(key=lambda e: e[0])
    gens = []
    rnd = 0
    while gens or pending:
        while pending and pending[0][0] <= rnd:
            gens.extend([g, 0] for g in pending.pop(0)[1])
        alive = []
        for entry in gens:
            if entry[1] > 0:
                entry[1] -= 1
                alive.append(entry)
                continue
            try:
                wait = next(entry[0])
                entry[1] = (wait or 1) - 1
                alive.append(entry)
            except StopIteration:
                pass
        gens = alive
        rnd += 1


def _mixers(p_rw, p_lru, p_gla, params, layer, bsz, seq, tb, nseq):
    nt = seq // tb
    blk = lambda c: pl.BlockSpec((nseq, tb, c), lambda b, t: (b, t, 0))
    nslab = p_lru.shape[0]
    p_rw = p_rw.reshape(bsz, seq, p_rw.shape[-1])
    p_gla = p_gla.reshape(bsz, seq, p_gla.shape[-1])
    p_lru = p_lru.reshape(nslab, bsz, seq, LANES)
    widths = (RW_WIDTH, LRU_WIDTH, GLA_WIDTH)
    outs = pl.pallas_call(
        functools.partial(_mixers_kernel, tb=tb, nseq=nseq, stagger=MIXER_STAGGER),
        grid=(bsz // nseq, nt),
        in_specs=[blk(p_rw.shape[-1]),
                  pl.BlockSpec((nslab, nseq, tb, LANES), lambda b, t: (0, b, t, 0)),
                  blk(p_gla.shape[-1])] + [_layer_spec(q, layer) for q in params],
        out_specs=[blk(w) for w in widths],
        out_shape=[jax.ShapeDtypeStruct((bsz, seq, w), F32) for w in widths],
        scratch_shapes=[pltpu.VMEM((nseq, SUBLANES, p_rw.shape[-1]), F32),
                        pltpu.VMEM((nseq, 2, LANES, LANES), F32),
                        pltpu.VMEM((nseq, (LRU_CONV - 1) * SUBLANES, LRU_WIDTH), F32),
                        pltpu.VMEM((nseq, 1, LRU_WIDTH), F32),
                        pltpu.VMEM((nseq, LRU_WIDTH // LANES, tb, LANES), F32),
                        pltpu.VMEM((nseq, GLA_WIDTH, GLA_KEY_WIDTH), F32)],
        compiler_params=pltpu.CompilerParams(dimension_semantics=("arbitrary", "arbitrary"),
                                             vmem_limit_bytes=VMEM_LIMIT),
        name="mixers",
    )(p_rw, p_lru, p_gla, *params)
    return [y.reshape(bsz * seq, y.shape[-1]) for y in outs]


def _post_kernel(yrw_ref, ylru_ref, ygla_ref, x_ref, wout_ref, g2_ref, wg_ref, wu_ref, wd_ref,
                 gf_ref, o_ref, *, ff_chunk, final):
    y = jnp.concatenate([yrw_ref[...], ylru_ref[...], ygla_ref[...]], axis=1).astype(BF16)
    x = x_ref[...] + jnp.dot(y, wout_ref[...], preferred_element_type=F32)
    hn = _rms_rows(x, g2_ref[...]).astype(BF16)
    d_ff = wg_ref.shape[1]
    acc = x
    for c in range(d_ff // ff_chunk):
        cols = slice(c * ff_chunk, (c + 1) * ff_chunk)
        gate = jnp.dot(hn, wg_ref[:, cols], preferred_element_type=F32)
        up = jnp.dot(hn, wu_ref[:, cols], preferred_element_type=F32)
        h = (gate * _sigmoid(gate) * up).astype(BF16)
        acc = acc + jnp.dot(h, wd_ref[cols, :], preferred_element_type=F32)
    if final:
        acc = _rms_rows(acc, gf_ref[...])
    o_ref[...] = acc


def _post(y_rw, y_lru, y_gla, xf, w_out, g2, w_gate, w_up, w_down, gf, layer, tm, final):
    n, d = xf.shape
    row = lambda c: pl.BlockSpec((tm, c), lambda i: (i, 0))
    stacked = (w_out, g2, w_gate, w_up, w_down)
    return pl.pallas_call(
        functools.partial(_post_kernel, ff_chunk=256, final=final),
        grid=(n // tm,),
        in_specs=[row(y_rw.shape[1]), row(y_lru.shape[1]), row(y_gla.shape[1]), row(d)]
        + [_layer_spec(q, layer) for q in stacked] + [_const_spec(gf.shape)],
        out_specs=row(d),
        out_shape=jax.ShapeDtypeStruct((n, d), F32),
        compiler_params=pltpu.CompilerParams(dimension_semantics=("arbitrary",),
                                             vmem_limit_bytes=VMEM_LIMIT),
        name="post",
    )(y_rw, y_lru, y_gla, xf, *stacked, gf)


def _rows(v):
    return v.reshape(v.shape[0], 1, -1).astype(F32)


def _block_diag_groups(w, per_group):
    nl, nb, c, _ = w.shape
    groups = nb // per_group
    w = w.reshape(nl, groups, per_group, c, c)
    eye = jnp.eye(per_group, dtype=w.dtype)
    out = jnp.einsum("lgbij,bk->lgbikj", w, eye)
    return out.reshape(nl, groups, per_group * c, per_group * c)


def kernel(x, norm1_g, w_in, rw_mu, rw_w0, rw_w_up, rw_a0, rw_a_up, rw_g_up, rw_k_k, rw_k_a, rw_r_k, rw_ln_g, rw_ln_b, lru_conv_w, lru_conv_b, lru_wa, lru_ba, lru_wx, lru_bx, lru_lam, lru_norm_g, gla_gk_up, gla_gk_b, gla_norm_g, w_out, norm2_g, ffn_w_gate, ffn_w_up, ffn_w_down, final_norm_g):
    bsz, seq, d = x.shape
    depth = w_in.shape[0]
    n = bsz * seq
    tb = 256
    tm = 512 if n % 512 == 0 else tb
    nseq = 2 if bsz % 2 == 0 else 1

    w_in_b = w_in.astype(BF16)
    pad0 = ((0, 0),)
    mixer_params = (
        _rows(rw_mu), _rows(rw_w0),
        jnp.pad(rw_w_up, pad0 + ((0, HEAD), (0, 0))).astype(BF16),
        _rows(rw_a0),
        jnp.pad(rw_a_up, pad0 + ((HEAD, 0), (0, 0))).astype(BF16),
        rw_g_up.astype(BF16), _rows(rw_k_k), _rows(rw_k_a), _rows(rw_r_k), _rows(rw_ln_g), _rows(rw_ln_b),
        lru_conv_w.astype(F32), _rows(lru_conv_b),
        _block_diag_groups(lru_wa, 4).astype(BF16), _rows(lru_ba),
        _block_diag_groups(lru_wx, 4).astype(BF16), _rows(lru_bx), _rows(lru_lam), _rows(lru_norm_g),
        jnp.pad(gla_gk_up, pad0 + ((0, LANES - GLA_GATE_RANK), (0, 0))).astype(F32), _rows(gla_gk_b),
        _rows(jnp.tile(gla_norm_g, (1, GLA_WIDTH // HEAD))))
    g1, g2 = _rows(norm1_g), _rows(norm2_g)
    w_out_b, w_gate_b, w_up_b, w_down_b = (w.astype(BF16) for w in (w_out, ffn_w_gate, ffn_w_up, ffn_w_down))
    gf = final_norm_g.reshape(1, -1).astype(F32)

    xf = x.reshape(n, d)
    for l in range(depth):
        p_rw, p_lru, p_gla = _inproj(xf, g1, w_in_b, l, tm, tb)
        y_rw, y_lru, y_gla = _mixers(p_rw, p_lru, p_gla, mixer_params, l, bsz, seq, tb, nseq)
        xf = _post(y_rw, y_lru, y_gla, xf, w_out_b, g2, w_gate_b, w_up_b, w_down_b, gf, l, tm,
                   final=(l == depth - 1))
    return xf.reshape(bsz, seq, d)
```

```python
import functools
import math

import jax
import jax.numpy as jnp
from jax import lax
from jax.experimental import pallas as pl
from jax.experimental.pallas import tpu as pltpu

F32 = jnp.float32
BF16 = jnp.bfloat16

HEAD = 64
NORM_EPS = 1e-6
RW_WIDTH = 256
RW_DECAY_SCALE = math.exp(-0.5)
RW_LN_EPS = 64e-5
LRU_WIDTH = 512
LRU_CONV = 4
LRU_C = 8.0
GLA_WIDTH = 256
GLA_KEY_WIDTH = 128
GLA_DK = 32
GLA_GATE_RANK = 16
GLA_GATE_NORM = 16.0
GLA_PAD = 896

CHUNK = 64
LANES = 128
SUBLANES = 8
VMEM_LIMIT = 56 * 1024 * 1024


def _dot(a, b):
    return jnp.dot(a.astype(BF16), b.astype(BF16), preferred_element_type=F32)


def _dot_nt(a, b):
    return lax.dot_general(a.astype(BF16), b.astype(BF16), (((1,), (1,)), ((), ())),
                           preferred_element_type=F32)


def _dot_tn(a, b):
    return lax.dot_general(a.astype(BF16), b.astype(BF16), (((0,), (0,)), ((), ())),
                           preferred_element_type=F32)


def _split(x):
    hi = x.astype(BF16)
    lo = (x - hi.astype(F32)).astype(BF16)
    return hi, lo


def _group_sum(x, ones):
    return jnp.dot(x.astype(BF16), ones, preferred_element_type=F32)


def _chunk_cumsums(tril_c, x):
    hi, lo = _split(x)
    cs = jnp.dot(tril_c, hi, preferred_element_type=F32) + jnp.dot(tril_c, lo, preferred_element_type=F32)
    nc = x.shape[0] // CHUNK
    cl = jnp.concatenate(
        [jnp.broadcast_to(cs[(c + 1) * CHUNK - 1:(c + 1) * CHUNK], (CHUNK, x.shape[1])) for c in range(nc)],
        axis=0)
    return cs, cl


def _dot3(a, b):
    ah, al = _split(a)
    bh, bl = _split(b)
    return (jnp.dot(ah, bh, preferred_element_type=F32)
            + jnp.dot(al, bh, preferred_element_type=F32)
            + jnp.dot(ah, bl, preferred_element_type=F32))


def _iota(shape, dim):
    return lax.broadcasted_iota(jnp.int32, shape, dim)


def _group_ones(n, group):
    r = _iota((n, n), 0) // group
    c = _iota((n, n), 1) // group
    return jnp.where(r == c, 1.0, 0.0).astype(BF16)


def _chunk_tril_ones(tb):
    tr = _iota((tb, tb), 0)
    tc = _iota((tb, tb), 1)
    return jnp.where(((tr // CHUNK) == (tc // CHUNK)) & (tr >= tc), 1.0, 0.0).astype(BF16)


def _rms_rows(x, g):
    ms = jnp.mean(x * x, axis=-1, keepdims=True)
    return x * lax.rsqrt(ms + NORM_EPS) * g


def _shift_rows(x, prev_tail, j):
    xs = pltpu.roll(x, j, 0)
    fix = pltpu.roll(prev_tail, j, 0)
    row = _iota((SUBLANES, x.shape[1]), 0)
    head = jnp.where(row < j, fix, xs[0:SUBLANES])
    return jnp.concatenate([head, xs[SUBLANES:]], axis=0)


def _sigmoid(x):
    return 1.0 / (1.0 + jnp.exp2(x * (-math.log2(math.e))))


def _sqrt_nonneg(y):
    return jnp.where(y > 0.0, y * lax.rsqrt(y), 0.0)


def _const_spec(shape):
    nd = len(shape)
    return pl.BlockSpec(shape, lambda *_: (0,) * nd)


def _layer_spec(arr, layer, single_buffer=False):
    nd = arr.ndim - 1
    mode = pl.Buffered(1) if single_buffer else None
    return pl.BlockSpec((None,) + arr.shape[1:], lambda *_: (layer,) + (0,) * nd, pipeline_mode=mode)


RW_PROJ_W = 1024


def _inproj_kernel(x_ref, g_ref, w_ref, prw_ref, plru_ref, pgla_ref, *, tm, tb):
    hn = _rms_rows(x_ref[...], g_ref[...]).astype(BF16)
    lru_w = plru_ref.shape[0] * LANES
    prw_ref[...] = jnp.dot(hn, w_ref[:, 0:RW_PROJ_W], preferred_element_type=F32)
    p_lru = jnp.dot(hn, w_ref[:, RW_PROJ_W:RW_PROJ_W + lru_w], preferred_element_type=F32)
    seg = tb // SUBLANES
    for blk in range(tm // tb):
        for s in range(SUBLANES):
            rows = slice(blk * tb + s * seg, blk * tb + (s + 1) * seg)
            for c in range(lru_w // LANES):
                plru_ref[c, pl.ds(blk * tb + s, seg, stride=SUBLANES), :] = (
                    p_lru[rows, c * LANES:(c + 1) * LANES])
    gla0 = RW_PROJ_W + lru_w
    qkv_w = 2 * GLA_KEY_WIDTH + GLA_WIDTH
    pgla_ref[:, 0:qkv_w] = jnp.dot(hn, w_ref[:, gla0:gla0 + qkv_w], preferred_element_type=F32)
    rest = jnp.dot(hn, w_ref[:, gla0 + qkv_w:], preferred_element_type=F32)
    pgla_ref[:, qkv_w:qkv_w + GLA_WIDTH] = rest[:, GLA_GATE_RANK:GLA_GATE_RANK + GLA_WIDTH]
    pgla_ref[:, qkv_w + GLA_WIDTH:] = rest[:, 0:LANES]


def _inproj(xf, g, w, layer, tm, tb):
    n, d = xf.shape
    lru_w = 2 * LRU_WIDTH
    gla_w = GLA_PAD
    row = lambda c: pl.BlockSpec((tm, c), lambda i: (i, 0))
    return pl.pallas_call(
        functools.partial(_inproj_kernel, tm=tm, tb=tb),
        grid=(n // tm,),
        in_specs=[row(d), _layer_spec(g, layer), _layer_spec(w, layer)],
        out_specs=[row(RW_PROJ_W), pl.BlockSpec((lru_w // LANES, tm, LANES), lambda i: (0, i, 0)), row(gla_w)],
        out_shape=[jax.ShapeDtypeStruct((n, RW_PROJ_W), F32),
                   jax.ShapeDtypeStruct((lru_w // LANES, n, LANES), F32),
                   jax.ShapeDtypeStruct((n, gla_w), F32)],
        compiler_params=pltpu.CompilerParams(dimension_semantics=("arbitrary",),
                                             vmem_limit_bytes=VMEM_LIMIT),
        name="inproj",
    )(xf, g, w)


def _rwkv_steps(p_ref, prm, y_ref, tail_ref, s_ref, tb):
    (mu_ref, w0_ref, wup_ref, a0_ref, aup_ref, gup_ref, kk_ref, ka_ref, rk_ref, lng_ref,
     lnb_ref) = prm
    p = p_ref[...]
    shifted = _shift_rows(p, tail_ref[...], 1)
    tail_ref[...] = p[tb - SUBLANES:tb, :]
    ps = p + (shifted - p) * mu_ref[...]
    r = ps[:, 0:256]
    k = ps[:, 256:512]
    v = ps[:, 512:768]
    lo_wa = ps[:, 768:896]
    g_lo = ps[:, 896:1024]
    ones_h = _group_ones(RW_WIDTH, HEAD)
    kk = k * kk_ref[...]
    w_raw = w0_ref[...] + _dot(jnp.tanh(lo_wa), wup_ref[...])
    a = _sigmoid(a0_ref[...] + _dot(lo_wa, aup_ref[...]))
    g = _dot(_sigmoid(g_lo), gup_ref[...])
    kk_sq = _group_sum(kk * kk, ones_h)
    yield
    lw = -RW_DECAY_SCALE * _sigmoid(w_raw)
    kk = kk * lax.rsqrt(jnp.maximum(kk_sq, 1e-24))
    k = k * (1.0 + (a - 1.0) * ka_ref[...])
    a_neg = -kk
    b = kk * a

    cs, cl = _chunk_cumsums(_chunk_tril_ones(tb), lw)
    yield
    inv = jnp.exp(-cs)
    dec = jnp.exp(cl - cs)
    wl = jnp.exp(cl)
    at, rt, bt, kt = a_neg * jnp.exp(cs - lw), r * jnp.exp(cs), b * inv, k * inv
    bd, kd = b * dec, k * dec

    rr = _iota((LANES, LANES), 0)
    cc = _iota((LANES, LANES), 1)
    same = (rr // CHUNK) == (cc // CHUNK)
    strict = same & (cc < rr)
    incl = same & (cc <= rr)
    eye = jnp.where(rr == cc, 1.0, 0.0).astype(F32)
    m0 = _iota((CHUNK, LANES), 1) < HEAD
    zeros = jnp.zeros((LANES, LANES), F32)

    def stack(x):
        return jnp.concatenate([jnp.where(m0, x, 0.0), jnp.where(m0, 0.0, x)], axis=0)

    def tile(x):
        return jnp.concatenate([x, x], axis=0)

    nc = tb // CHUNK
    idx = [(c, pair) for c in range(nc) for pair in range(2)]

    def blk(x, c, pair):
        return x[c * CHUNK:(c + 1) * CHUNK, pair * LANES:(pair + 1) * LANES]

    a_st = [stack(blk(at, *i)) for i in idx]
    ar = [jnp.concatenate([a_st[j], stack(blk(rt, *i))], axis=0) for j, i in enumerate(idx)]
    gm = [_dot_nt(ar[j], jnp.concatenate([tile(blk(bt, *i)), tile(blk(kt, *i))], axis=0))
          for j, i in enumerate(idx)]
    yield
    n_mat = [jnp.where(strict, m[0:LANES, 0:LANES], 0.0) for m in gm]
    l_ak = [jnp.where(strict, m[0:LANES, LANES:], 0.0) for m in gm]
    l_rbk = [jnp.concatenate([jnp.where(incl, m[LANES:, 0:LANES], 0.0),
                              jnp.where(incl, m[LANES:, LANES:], 0.0)], axis=1) for m in gm]
    v_bd = [stack(blk(v, *i)) for i in idx]
    lv = [_dot(l, vb) for l, vb in zip(l_ak, v_bd)]

    d4 = (rr // 4) == (cc // 4)
    nd = [jnp.where(d4, m, 0.0) for m in n_mat]
    nd2 = [_dot(m, m) for m in nd]
    yield
    t_inv = [eye + m + _dot(m2, eye + m) for m, m2 in zip(nd, nd2)]
    yield
    s = 4
    while s < CHUNK:
        off = ((rr // (2 * s)) == (cc // (2 * s))) & ((rr // s) != (cc // s))
        if s < SUBLANES:
            nt = [_dot(jnp.where(off, m, 0.0), t) for m, t in zip(n_mat, t_inv)]
            yield
            t_inv = [t + _dot(t, m) for t, m in zip(t_inv, nt)]
            yield
        else:
            def low(m):
                return jnp.concatenate([m[k + s:k + 2 * s] for k in range(0, LANES, 2 * s)], axis=0)

            il = _iota((LANES // 2, LANES), 0)
            rl = (il // s) * (2 * s) + s + il % s
            cl_ = _iota((LANES // 2, LANES), 1)
            off_low = ((rl // (2 * s)) == (cl_ // (2 * s))) & ((rl // s) != (cl_ // s))
            zs = jnp.zeros((s, LANES), F32)
            nt = [_dot(jnp.where(off_low, low(m), 0.0), t) for m, t in zip(n_mat, t_inv)]
            yield
            upd = []
            for t, m in zip(t_inv, nt):
                m_full = jnp.concatenate(
                    [piece for k in range(LANES // (2 * s)) for piece in (zs, m[k * s:(k + 1) * s])], axis=0)
                upd.append(_dot(low(t), m_full))
            yield
            t_inv = [jnp.concatenate(
                [piece for k in range(LANES // (2 * s))
                 for piece in (t[2 * k * s:2 * k * s + s], t[2 * k * s + s:2 * (k + 1) * s] + u[k * s:(k + 1) * s])],
                axis=0) for t, u in zip(t_inv, upd)]
        s *= 2

    x = [_dot(t, jnp.concatenate([a_, l], axis=1)) for t, a_, l in zip(t_inv, a_st, lv)]
    yield
    z = [jnp.concatenate([xx, jnp.concatenate([zeros, vb], axis=1)], axis=0) for xx, vb in zip(x, v_bd)]
    y2 = [_dot(l, zz) for l, zz in zip(l_rbk, z)]
    ms = [_dot_tn(zz, jnp.concatenate([stack(blk(bd, *i)), stack(blk(kd, *i))], axis=0))
          for zz, i in zip(z, idx)]
    yield

    outs = []
    states = [s_ref[0], s_ref[1]]
    for c in range(nc):
        halves = []
        for pair in range(2):
            j = 2 * c + pair
            state = states[pair]
            y_bd = _dot_nt(ar[j][LANES:] + y2[j][:, 0:LANES], state) + y2[j][:, LANES:]
            halves.append(y_bd[0:CHUNK] + y_bd[CHUNK:])
            states[pair] = (state * blk(wl, c, pair)[0:1] + _dot(state, ms[j][0:LANES])
                            + ms[j][LANES:])
        outs.append(jnp.concatenate(halves, axis=1))
        yield
    s_ref[0] = states[0]
    s_ref[1] = states[1]

    y = jnp.concatenate(outs, axis=0)
    mean = _group_sum(y, ones_h) * (1.0 / HEAD)
    bonus_s = _group_sum(r * k * rk_ref[...], ones_h)
    yield
    d = y - mean
    var = _group_sum(d * d, ones_h) * (1.0 / HEAD)
    yield
    yn = d * lax.rsqrt(var + RW_LN_EPS) * lng_ref[...] + lnb_ref[...]
    y_ref[...] = (yn + bonus_s * v) * g


def _lru_steps(slab, prm, y_ref, tail_ref, h_ref, unperm_ref, tb):
    cw_ref, cb_ref, wa_ref, ba_ref, wx_ref, bx_ref, lam_ref, ng_ref = prm
    nslab = LRU_WIDTH // LANES
    seg = tb // SUBLANES
    xb = jnp.concatenate([slab(c) for c in range(nslab)], axis=1)
    gate = jnp.concatenate([slab(nslab + c) for c in range(nslab)], axis=1)
    sub0 = _iota((SUBLANES, LRU_WIDTH), 0) == 0

    def group(x, g):
        return x[g * SUBLANES:(g + 1) * SUBLANES]

    wrapped = []
    for g in range(seg - (LRU_CONV - 1), seg):
        prev = tail_ref[(g - seg + LRU_CONV - 1) * SUBLANES:(g - seg + LRU_CONV) * SUBLANES, :]
        wrapped.append(jnp.where(sub0, pltpu.roll(prev, 1, 0), pltpu.roll(group(xb, g), 1, 0)))
    tail_ref[...] = xb[(seg - (LRU_CONV - 1)) * SUBLANES:, :]
    xc = cb_ref[...] + xb * cw_ref[LRU_CONV - 1:LRU_CONV, :]
    for j in range(1, LRU_CONV):
        shifted = jnp.concatenate(wrapped[LRU_CONV - 1 - j:] + [xb[0:(seg - j) * SUBLANES]], axis=0)
        xc = xc + shifted * cw_ref[LRU_CONV - 1 - j:LRU_CONV - j, :]

    half = LRU_WIDTH // 2
    yield LRU_WAITS[0]
    z_r = jnp.concatenate([_dot(xc[:, 0:half], wa_ref[0]), _dot(xc[:, half:], wa_ref[1])], axis=1)
    z_i = jnp.concatenate([_dot(xc[:, 0:half], wx_ref[0]), _dot(xc[:, half:], wx_ref[1])], axis=1)
    yield
    gate_r = _sigmoid(z_r + ba_ref[...])
    gate_i = _sigmoid(z_i + bx_ref[...])
    lam = lam_ref[...]
    softplus_neg_lam = jnp.maximum(-lam, 0.0) + jnp.log1p(jnp.exp(-jnp.abs(lam)))
    log_a = (-LRU_C * gate_r) * softplus_neg_lam
    a = jnp.exp(log_a)
    u = _sqrt_nonneg(1.0 - a * a) * (gate_i * xc)

    h0 = [group(u, 0)]
    ap = [group(a, 0)]
    for g in range(1, seg):
        h0.append(group(a, g) * h0[-1] + group(u, g))
        ap.append(group(a, g) * ap[-1])
    h_end, a_end = h0[-1], ap[-1]
    row = _iota((SUBLANES, LRU_WIDTH), 0)
    for d in (1, 2, 4):
        keep = row >= d
        h_end = a_end * jnp.where(keep, pltpu.roll(h_end, d, 0), 0.0) + h_end
        a_end = a_end * jnp.where(keep, pltpu.roll(a_end, d, 0), 1.0)
    h_true = h_end + a_end * h_ref[...]
    h_in = jnp.where(sub0, h_ref[...], pltpu.roll(h_true, 1, 0))
    h_ref[...] = h_true[SUBLANES - 1:SUBLANES, :]
    h = jnp.concatenate([h0[g] + ap[g] * h_in for g in range(seg)], axis=0)

    c2 = 2.0 * math.sqrt(2.0 / math.pi)
    y = h * (gate * _sigmoid(gate * (c2 + (c2 * 0.044715) * (gate * gate))))
    ones_h = _group_ones(half, HEAD)
    ysq = y * y
    yield LRU_WAITS[1]
    ms = jnp.concatenate([_group_sum(ysq[:, 0:half], ones_h),
                          _group_sum(ysq[:, half:], ones_h)], axis=1) * (1.0 / HEAD)
    yield
    y = y * lax.rsqrt(ms + NORM_EPS) * ng_ref[...]
    for c in range(nslab):
        unperm_ref[c] = y[:, c * LANES:(c + 1) * LANES]
    for s in range(SUBLANES):
        y_ref[s * seg:(s + 1) * seg, :] = jnp.concatenate(
            [unperm_ref[c, pl.ds(s, seg, stride=SUBLANES), :] for c in range(nslab)], axis=1)


def _gla_steps(p_ref, prm, y_ref, s_ref, tb):
    gkup_ref, gkb_ref, ng_ref = prm
    kw = GLA_KEY_WIDTH
    q = p_ref[:, 0:kw] * (GLA_DK ** -0.5)
    k = p_ref[:, kw:2 * kw]
    v = p_ref[:, 2 * kw:2 * kw + GLA_WIDTH]
    x = _dot3(p_ref[:, 768:GLA_PAD], gkup_ref[...]) + gkb_ref[...]
    yield
    la = (jnp.minimum(x, 0.0) - jnp.log1p(jnp.exp(-jnp.abs(x)))) * (1.0 / GLA_GATE_NORM)

    bc, bl = _chunk_cumsums(_chunk_tril_ones(tb), la)
    yield
    qi = q * jnp.exp(bc)
    ki = k * jnp.exp(-bc)
    kd = k * jnp.exp(bl - bc)
    dec = jnp.exp(bl)

    nh = GLA_WIDTH // HEAD
    lane_k = _iota((tb, kw), 1) // GLA_DK
    lane_v = _iota((tb, GLA_WIDTH), 1) // HEAD
    qs = jnp.concatenate([jnp.where(lane_k == h, qi, 0.0) for h in range(nh)], axis=0)
    st = _iota((nh * tb, tb), 0) % tb
    sj = _iota((nh * tb, tb), 1)
    causal = ((st // CHUNK) == (sj // CHUNK)) & (st >= sj)
    sc = _dot_nt(qs, ki)
    diag = (_iota((GLA_WIDTH, kw), 0) // HEAD) == (_iota((GLA_WIDTH, kw), 1) // GLA_DK)
    nc = tb // CHUNK
    upd = [jnp.where(diag, _dot_tn(v[c * CHUNK:(c + 1) * CHUNK], kd[c * CHUNK:(c + 1) * CHUNK]), 0.0)
           for c in range(nc)]
    yield
    o_full = _dot(jnp.where(causal, sc, 0.0), v)
    yield
    o_intra = jnp.where(lane_v == 0, o_full[0:tb], 0.0)
    for h in range(1, nh):
        o_intra = o_intra + jnp.where(lane_v == h, o_full[h * tb:(h + 1) * tb], 0.0)

    state = s_ref[...]
    outs = []
    for c in range(nc):
        rows = slice(c * CHUNK, (c + 1) * CHUNK)
        outs.append(o_intra[rows] + _dot_nt(qi[rows], state))
        state = state * dec[c * CHUNK:c * CHUNK + 1, :] + upd[c]
    s_ref[...] = state

    o = jnp.concatenate(outs, axis=0)
    ms = _group_sum(o * o, _group_ones(GLA_WIDTH, HEAD)) * (1.0 / HEAD)
    yield
    gate = p_ref[:, 512:768]
    y_ref[...] = o * lax.rsqrt(ms + NORM_EPS) * ng_ref[...] * (gate * _sigmoid(gate))


MIXER_STAGGER = (3, 3)
LRU_WAITS = (2, 6)
N_RW_PARAMS = 11
N_LRU_PARAMS = 8
N_GLA_PARAMS = 3


def _mixers_kernel(*refs, tb, nseq, stagger):
    prw_ref, plru_ref, pgla_ref = refs[0:3]
    o = 3
    rw_prm = refs[o:o + N_RW_PARAMS]
    o += N_RW_PARAMS
    lru_prm = refs[o:o + N_LRU_PARAMS]
    o += N_LRU_PARAMS
    gla_prm = refs[o:o + N_GLA_PARAMS]
    o += N_GLA_PARAMS
    yrw_ref, ylru_ref, ygla_ref = refs[o:o + 3]
    rw_tail, rw_state, lru_tail, lru_h, lru_unperm, gla_state = refs[o + 3:]

    @pl.when(pl.program_id(1) == 0)
    def _():
        for ref in (rw_tail, rw_state, lru_tail, lru_h, gla_state):
            ref[...] = jnp.zeros_like(ref)

    pending = [(0, [_rwkv_steps(prw_ref.at[q], rw_prm, yrw_ref.at[q], rw_tail.at[q], rw_state.at[q], tb)
                    for q in range(nseq)]),
               (stagger[0], [_gla_steps(pgla_ref.at[q], gla_prm, ygla_ref.at[q], gla_state.at[q], tb)
                             for q in range(nseq)]),
               (stagger[1], [_lru_steps(lambda c, q=q: plru_ref[c, q], lru_prm, ylru_ref.at[q],
                                        lru_tail.at[q], lru_h.at[q], lru_unperm.at[q], tb)
                             for q in range(nseq)])]
    pending.sort(key=lambda e: e[0])
    gens = []
    rnd = 0
    while gens or pending:
        while pending and pending[0][0] <= rnd:
            gens.extend([g, 0] for g in pending.pop(0)[1])
        alive = []
        for entry in gens:
            if entry[1] > 0:
                entry[1] -= 1
                alive.append(entry)
                continue
            try:
                wait = next(entry[0])
                entry[1] = (wait or 1) - 1
                alive.append(entry)
            except StopIteration:
                pass
        gens = alive
        rnd += 1


def _mixers(p_rw, p_lru, p_gla, params, layer, bsz, seq, tb, nseq):
    nt = seq // tb
    blk = lambda c: pl.BlockSpec((nseq, tb, c), lambda b, t: (b, t, 0))
    nslab = p_lru.shape[0]
    p_rw = p_rw.reshape(bsz, seq, p_rw.shape[-1])
    p_gla = p_gla.reshape(bsz, seq, p_gla.shape[-1])
    p_lru = p_lru.reshape(nslab, bsz, seq, LANES)
    widths = (RW_WIDTH, LRU_WIDTH, GLA_WIDTH)
    outs = pl.pallas_call(
        functools.partial(_mixers_kernel, tb=tb, nseq=nseq, stagger=MIXER_STAGGER),
        grid=(bsz // nseq, nt),
        in_specs=[blk(p_rw.shape[-1]),
                  pl.BlockSpec((nslab, nseq, tb, LANES), lambda b, t: (0, b, t, 0)),
                  blk(p_gla.shape[-1])] + [_layer_spec(q, layer) for q in params],
        out_specs=[blk(w) for w in widths],
        out_shape=[jax.ShapeDtypeStruct((bsz, seq, w), F32) for w in widths],
        scratch_shapes=[pltpu.VMEM((nseq, SUBLANES, p_rw.shape[-1]), F32),
                        pltpu.VMEM((nseq, 2, LANES, LANES), F32),
                        pltpu.VMEM((nseq, (LRU_CONV - 1) * SUBLANES, LRU_WIDTH), F32),
                        pltpu.VMEM((nseq, 1, LRU_WIDTH), F32),
                        pltpu.VMEM((nseq, LRU_WIDTH // LANES, tb, LANES), F32),
                        pltpu.VMEM((nseq, GLA_WIDTH, GLA_KEY_WIDTH), F32)],
        compiler_params=pltpu.CompilerParams(dimension_semantics=("arbitrary", "arbitrary"),
                                             vmem_limit_bytes=VMEM_LIMIT),
        name="mixers",
    )(p_rw, p_lru, p_gla, *params)
    return [y.reshape(bsz * seq, y.shape[-1]) for y in outs]


def _post_kernel(yrw_ref, ylru_ref, ygla_ref, x_ref, wout_ref, g2_ref, wg_ref, wu_ref, wd_ref,
                 gf_ref, o_ref, *, ff_chunk, final):
    y = jnp.concatenate([yrw_ref[...], ylru_ref[...], ygla_ref[...]], axis=1).astype(BF16)
    x = x_ref[...] + jnp.dot(y, wout_ref[...], preferred_element_type=F32)
    hn = _rms_rows(x, g2_ref[...]).astype(BF16)
    d_ff = wg_ref.shape[1]
    acc = x
    for c in range(d_ff // ff_chunk):
        cols = slice(c * ff_chunk, (c + 1) * ff_chunk)
        gate = jnp.dot(hn, wg_ref[:, cols], preferred_element_type=F32)
        up = jnp.dot(hn, wu_ref[:, cols], preferred_element_type=F32)
        h = (gate * _sigmoid(gate) * up).astype(BF16)
        acc = acc + jnp.dot(h, wd_ref[cols, :], preferred_element_type=F32)
    if final:
        acc = _rms_rows(acc, gf_ref[...])
    o_ref[...] = acc


def _post(y_rw, y_lru, y_gla, xf, w_out, g2, w_gate, w_up, w_down, gf, layer, tm, final):
    n, d = xf.shape
    row = lambda c: pl.BlockSpec((tm, c), lambda i: (i, 0))
    stacked = (w_out, g2, w_gate, w_up, w_down)
    return pl.pallas_call(
        functools.partial(_post_kernel, ff_chunk=256, final=final),
        grid=(n // tm,),
        in_specs=[row(y_rw.shape[1]), row(y_lru.shape[1]), row(y_gla.shape[1]), row(d)]
        + [_layer_spec(q, layer, single_buffer=q.ndim == 3 and q.shape[1] > 1) for q in stacked]
        + [_const_spec(gf.shape)],
        out_specs=row(d),
        out_shape=jax.ShapeDtypeStruct((n, d), F32),
        compiler_params=pltpu.CompilerParams(dimension_semantics=("arbitrary",),
                                             vmem_limit_bytes=VMEM_LIMIT),
        name="post",
    )(y_rw, y_lru, y_gla, xf, *stacked, gf)


def _rows(v):
    return v.reshape(v.shape[0], 1, -1).astype(F32)


def _block_diag_groups(w, per_group):
    nl, nb, c, _ = w.shape
    groups = nb // per_group
    w = w.reshape(nl, groups, per_group, c, c)
    eye = jnp.eye(per_group, dtype=w.dtype)
    out = jnp.einsum("lgbij,bk->lgbikj", w, eye)
    return out.reshape(nl, groups, per_group * c, per_group * c)


def kernel(x, norm1_g, w_in, rw_mu, rw_w0, rw_w_up, rw_a0, rw_a_up, rw_g_up, rw_k_k, rw_k_a, rw_r_k, rw_ln_g, rw_ln_b, lru_conv_w, lru_conv_b, lru_wa, lru_ba, lru_wx, lru_bx, lru_lam, lru_norm_g, gla_gk_up, gla_gk_b, gla_norm_g, w_out, norm2_g, ffn_w_gate, ffn_w_up, ffn_w_down, final_norm_g):
    bsz, seq, d = x.shape
    depth = w_in.shape[0]
    n = bsz * seq
    tb = 256
    tm = 1024 if n % 1024 == 0 else tb
    tm_in = tm
    nseq = 2 if bsz % 2 == 0 else 1

    w_in_b = w_in.astype(BF16)
    pad0 = ((0, 0),)
    mixer_params = (
        _rows(rw_mu), _rows(rw_w0),
        jnp.pad(rw_w_up, pad0 + ((0, HEAD), (0, 0))).astype(BF16),
        _rows(rw_a0),
        jnp.pad(rw_a_up, pad0 + ((HEAD, 0), (0, 0))).astype(BF16),
        rw_g_up.astype(BF16), _rows(rw_k_k), _rows(rw_k_a), _rows(rw_r_k), _rows(rw_ln_g), _rows(rw_ln_b),
        lru_conv_w.astype(F32), _rows(lru_conv_b),
        _block_diag_groups(lru_wa, 4).astype(BF16), _rows(lru_ba),
        _block_diag_groups(lru_wx, 4).astype(BF16), _rows(lru_bx), _rows(lru_lam), _rows(lru_norm_g),
        jnp.pad(gla_gk_up, pad0 + ((0, LANES - GLA_GATE_RANK), (0, 0))).astype(F32), _rows(gla_gk_b),
        _rows(jnp.tile(gla_norm_g, (1, GLA_WIDTH // HEAD))))
    g1, g2 = _rows(norm1_g), _rows(norm2_g)
    w_out_b, w_gate_b, w_up_b, w_down_b = (w.astype(BF16) for w in (w_out, ffn_w_gate, ffn_w_up, ffn_w_down))
    gf = final_norm_g.reshape(1, -1).astype(F32)

    xf = x.reshape(n, d)
    for l in range(depth):
        p_rw, p_lru, p_gla = _inproj(xf, g1, w_in_b, l, tm_in, tb)
        y_rw, y_lru, y_gla = _mixers(p_rw, p_lru, p_gla, mixer_params, l, bsz, seq, tb, nseq)
        xf = _post(y_rw, y_lru, y_gla, xf, w_out_b, g2, w_gate_b, w_up_b, w_down_b, gf, l, tm,
                   final=(l == depth - 1))
    return xf.reshape(bsz, seq, d)
```

```python
import functools
import math

import jax
import jax.numpy as jnp
from jax import lax
from jax.experimental import pallas as pl
from jax.experimental.pallas import tpu as pltpu

F32 = jnp.float32
BF16 = jnp.bfloat16

HEAD = 64
NORM_EPS = 1e-6
RW_WIDTH = 256
RW_DECAY_SCALE = math.exp(-0.5)
RW_LN_EPS = 64e-5
LRU_WIDTH = 512
LRU_CONV = 4
LRU_C = 8.0
GLA_WIDTH = 256
GLA_KEY_WIDTH = 128
GLA_DK = 32
GLA_GATE_RANK = 16
GLA_GATE_NORM = 16.0
GLA_PAD = 896

CHUNK = 64
LANES = 128
SUBLANES = 8
VMEM_LIMIT = 56 * 1024 * 1024


def _dot(a, b):
    return jnp.dot(a.astype(BF16), b.astype(BF16), preferred_element_type=F32)


def _dot_nt(a, b):
    return lax.dot_general(a.astype(BF16), b.astype(BF16), (((1,), (1,)), ((), ())),
                           preferred_element_type=F32)


def _dot_tn(a, b):
    return lax.dot_general(a.astype(BF16), b.astype(BF16), (((0,), (0,)), ((), ())),
                           preferred_element_type=F32)


def _split(x):
    hi = x.astype(BF16)
    lo = (x - hi.astype(F32)).astype(BF16)
    return hi, lo


def _group_sum(x, ones):
    return jnp.dot(x.astype(BF16), ones, preferred_element_type=F32)


def _chunk_cumsums(tril_c, x):
    hi, lo = _split(x)
    cs = jnp.dot(tril_c, hi, preferred_element_type=F32) + jnp.dot(tril_c, lo, preferred_element_type=F32)
    nc = x.shape[0] // CHUNK
    cl = jnp.concatenate(
        [jnp.broadcast_to(cs[(c + 1) * CHUNK - 1:(c + 1) * CHUNK], (CHUNK, x.shape[1])) for c in range(nc)],
        axis=0)
    return cs, cl


def _dot3(a, b):
    ah, al = _split(a)
    bh, bl = _split(b)
    return (jnp.dot(ah, bh, preferred_element_type=F32)
            + jnp.dot(al, bh, preferred_element_type=F32)
            + jnp.dot(ah, bl, preferred_element_type=F32))


def _iota(shape, dim):
    return lax.broadcasted_iota(jnp.int32, shape, dim)


def _group_ones(n, group):
    r = _iota((n, n), 0) // group
    c = _iota((n, n), 1) // group
    return jnp.where(r == c, 1.0, 0.0).astype(BF16)


def _chunk_tril_ones(tb):
    tr = _iota((tb, tb), 0)
    tc = _iota((tb, tb), 1)
    return jnp.where(((tr // CHUNK) == (tc // CHUNK)) & (tr >= tc), 1.0, 0.0).astype(BF16)


def _rms_rows(x, g):
    ms = jnp.mean(x * x, axis=-1, keepdims=True)
    return x * lax.rsqrt(ms + NORM_EPS) * g


def _shift_rows(x, prev_tail, j):
    xs = pltpu.roll(x, j, 0)
    fix = pltpu.roll(prev_tail, j, 0)
    row = _iota((SUBLANES, x.shape[1]), 0)
    head = jnp.where(row < j, fix, xs[0:SUBLANES])
    return jnp.concatenate([head, xs[SUBLANES:]], axis=0)


def _sigmoid(x):
    return 1.0 / (1.0 + jnp.exp2(x * (-math.log2(math.e))))


def _sqrt_nonneg(y):
    return jnp.where(y > 0.0, y * lax.rsqrt(y), 0.0)


def _const_spec(shape):
    nd = len(shape)
    return pl.BlockSpec(shape, lambda *_: (0,) * nd)


def _layer_spec(arr, layer, single_buffer=False):
    nd = arr.ndim - 1
    mode = pl.Buffered(1) if single_buffer else None
    return pl.BlockSpec((None,) + arr.shape[1:], lambda *_: (layer,) + (0,) * nd, pipeline_mode=mode)


RW_PROJ_W = 1024


def _inproj_kernel(x_ref, g_ref, w32_ref, prw_ref, plru_ref, pgla_ref, w_ref, *, tm, tb):
    @pl.when(pl.program_id(0) == 0)
    def _():
        w_ref[...] = w32_ref[...].astype(BF16)

    hn = _rms_rows(x_ref[...], g_ref[...]).astype(BF16)
    lru_w = plru_ref.shape[0] * LANES
    prw_ref[...] = jnp.dot(hn, w_ref[:, 0:RW_PROJ_W], preferred_element_type=F32)
    p_lru = jnp.dot(hn, w_ref[:, RW_PROJ_W:RW_PROJ_W + lru_w], preferred_element_type=F32)
    seg = tb // SUBLANES
    for blk in range(tm // tb):
        for s in range(SUBLANES):
            rows = slice(blk * tb + s * seg, blk * tb + (s + 1) * seg)
            for c in range(lru_w // LANES):
                plru_ref[c, pl.ds(blk * tb + s, seg, stride=SUBLANES), :] = (
                    p_lru[rows, c * LANES:(c + 1) * LANES])
    gla0 = RW_PROJ_W + lru_w
    qkv_w = 2 * GLA_KEY_WIDTH + GLA_WIDTH
    pgla_ref[:, 0:qkv_w] = jnp.dot(hn, w_ref[:, gla0:gla0 + qkv_w], preferred_element_type=F32)
    rest = jnp.dot(hn, w_ref[:, gla0 + qkv_w:], preferred_element_type=F32)
    pgla_ref[:, qkv_w:qkv_w + GLA_WIDTH] = rest[:, GLA_GATE_RANK:GLA_GATE_RANK + GLA_WIDTH]
    pgla_ref[:, qkv_w + GLA_WIDTH:] = rest[:, 0:LANES]


def _inproj(xf, g, w, layer, tm, tb):
    n, d = xf.shape
    lru_w = 2 * LRU_WIDTH
    gla_w = GLA_PAD
    row = lambda c: pl.BlockSpec((tm, c), lambda i: (i, 0))
    return pl.pallas_call(
        functools.partial(_inproj_kernel, tm=tm, tb=tb),
        grid=(n // tm,),
        in_specs=[row(d), _layer_spec(g, layer), _layer_spec(w, layer, single_buffer=True)],
        out_specs=[row(RW_PROJ_W), pl.BlockSpec((lru_w // LANES, tm, LANES), lambda i: (0, i, 0)), row(gla_w)],
        out_shape=[jax.ShapeDtypeStruct((n, RW_PROJ_W), F32),
                   jax.ShapeDtypeStruct((lru_w // LANES, n, LANES), F32),
                   jax.ShapeDtypeStruct((n, gla_w), F32)],
        scratch_shapes=[pltpu.VMEM(w.shape[1:], BF16)],
        compiler_params=pltpu.CompilerParams(dimension_semantics=("arbitrary",),
                                             vmem_limit_bytes=VMEM_LIMIT),
        name="inproj",
    )(xf, g, w)


def _rwkv_steps(p_ref, prm, y_ref, tail_ref, s_ref, tb):
    (mu_ref, w0_ref, wup_ref, a0_ref, aup_ref, gup_ref, kk_ref, ka_ref, rk_ref, lng_ref,
     lnb_ref) = prm
    p = p_ref[...]
    shifted = _shift_rows(p, tail_ref[...], 1)
    tail_ref[...] = p[tb - SUBLANES:tb, :]
    ps = p + (shifted - p) * mu_ref[...]
    r = ps[:, 0:256]
    k = ps[:, 256:512]
    v = ps[:, 512:768]
    lo_wa = ps[:, 768:896]
    g_lo = ps[:, 896:1024]
    ones_h = _group_ones(RW_WIDTH, HEAD)
    kk = k * kk_ref[...]
    w_raw = w0_ref[...] + _dot(jnp.tanh(lo_wa), wup_ref[...])
    a = _sigmoid(a0_ref[...] + _dot(lo_wa, aup_ref[...]))
    g = _dot(_sigmoid(g_lo), gup_ref[...])
    kk_sq = _group_sum(kk * kk, ones_h)
    yield
    lw = -RW_DECAY_SCALE * _sigmoid(w_raw)
    kk = kk * lax.rsqrt(jnp.maximum(kk_sq, 1e-24))
    k = k * (1.0 + (a - 1.0) * ka_ref[...])
    a_neg = -kk
    b = kk * a

    cs, cl = _chunk_cumsums(_chunk_tril_ones(tb), lw)
    yield
    inv = jnp.exp(-cs)
    dec = jnp.exp(cl - cs)
    wl = jnp.exp(cl)
    at, rt, bt, kt = a_neg * jnp.exp(cs - lw), r * jnp.exp(cs), b * inv, k * inv
    bd, kd = b * dec, k * dec

    rr = _iota((LANES, LANES), 0)
    cc = _iota((LANES, LANES), 1)
    same = (rr // CHUNK) == (cc // CHUNK)
    strict = same & (cc < rr)
    incl = same & (cc <= rr)
    eye = jnp.where(rr == cc, 1.0, 0.0).astype(F32)
    m0 = _iota((CHUNK, LANES), 1) < HEAD
    zeros = jnp.zeros((LANES, LANES), F32)

    def stack(x):
        return jnp.concatenate([jnp.where(m0, x, 0.0), jnp.where(m0, 0.0, x)], axis=0)

    def tile(x):
        return jnp.concatenate([x, x], axis=0)

    nc = tb // CHUNK
    idx = [(c, pair) for c in range(nc) for pair in range(2)]

    def blk(x, c, pair):
        return x[c * CHUNK:(c + 1) * CHUNK, pair * LANES:(pair + 1) * LANES]

    a_st = [stack(blk(at, *i)) for i in idx]
    ar = [jnp.concatenate([a_st[j], stack(blk(rt, *i))], axis=0) for j, i in enumerate(idx)]
    gm = [_dot_nt(ar[j], jnp.concatenate([tile(blk(bt, *i)), tile(blk(kt, *i))], axis=0))
          for j, i in enumerate(idx)]
    yield
    n_mat = [jnp.where(strict, m[0:LANES, 0:LANES], 0.0) for m in gm]
    l_ak = [jnp.where(strict, m[0:LANES, LANES:], 0.0) for m in gm]
    l_rbk = [jnp.concatenate([jnp.where(incl, m[LANES:, 0:LANES], 0.0),
                              jnp.where(incl, m[LANES:, LANES:], 0.0)], axis=1) for m in gm]
    v_bd = [stack(blk(v, *i)) for i in idx]
    lv = [_dot(l, vb) for l, vb in zip(l_ak, v_bd)]

    d4 = (rr // 4) == (cc // 4)
    nd = [jnp.where(d4, m, 0.0) for m in n_mat]
    nd2 = [_dot(m, m) for m in nd]
    yield
    t_inv = [eye + m + _dot(m2, eye + m) for m, m2 in zip(nd, nd2)]
    yield
    s = 4
    while s < CHUNK:
        off = ((rr // (2 * s)) == (cc // (2 * s))) & ((rr // s) != (cc // s))
        if s < SUBLANES:
            nt = [_dot(jnp.where(off, m, 0.0), t) for m, t in zip(n_mat, t_inv)]
            yield
            t_inv = [t + _dot(t, m) for t, m in zip(t_inv, nt)]
            yield
        else:
            def low(m):
                return jnp.concatenate([m[k + s:k + 2 * s] for k in range(0, LANES, 2 * s)], axis=0)

            il = _iota((LANES // 2, LANES), 0)
            rl = (il // s) * (2 * s) + s + il % s
            cl_ = _iota((LANES // 2, LANES), 1)
            off_low = ((rl // (2 * s)) == (cl_ // (2 * s))) & ((rl // s) != (cl_ // s))
            zs = jnp.zeros((s, LANES), F32)
            nt = [_dot(jnp.where(off_low, low(m), 0.0), t) for m, t in zip(n_mat, t_inv)]
            yield
            upd = []
            for t, m in zip(t_inv, nt):
                m_full = jnp.concatenate(
                    [piece for k in range(LANES // (2 * s)) for piece in (zs, m[k * s:(k + 1) * s])], axis=0)
                upd.append(_dot(low(t), m_full))
            yield
            t_inv = [jnp.concatenate(
                [piece for k in range(LANES // (2 * s))
                 for piece in (t[2 * k * s:2 * k * s + s], t[2 * k * s + s:2 * (k + 1) * s] + u[k * s:(k + 1) * s])],
                axis=0) for t, u in zip(t_inv, upd)]
        s *= 2

    x = [_dot(t, jnp.concatenate([a_, l], axis=1)) for t, a_, l in zip(t_inv, a_st, lv)]
    yield
    z = [jnp.concatenate([xx, jnp.concatenate([zeros, vb], axis=1)], axis=0) for xx, vb in zip(x, v_bd)]
    y2 = [_dot(l, zz) for l, zz in zip(l_rbk, z)]
    ms = [_dot_tn(zz, jnp.concatenate([stack(blk(bd, *i)), stack(blk(kd, *i))], axis=0))
          for zz, i in zip(z, idx)]
    yield

    outs = []
    states = [s_ref[0], s_ref[1]]
    for c in range(nc):
        halves = []
        for pair in range(2):
            j = 2 * c + pair
            state = states[pair]
            y_bd = _dot_nt(ar[j][LANES:] + y2[j][:, 0:LANES], state) + y2[j][:, LANES:]
            halves.append(y_bd[0:CHUNK] + y_bd[CHUNK:])
            states[pair] = (state * blk(wl, c, pair)[0:1] + _dot(state, ms[j][0:LANES])
                            + ms[j][LANES:])
        outs.append(jnp.concatenate(halves, axis=1))
        yield
    s_ref[0] = states[0]
    s_ref[1] = states[1]

    y = jnp.concatenate(outs, axis=0)
    mean = _group_sum(y, ones_h) * (1.0 / HEAD)
    bonus_s = _group_sum(r * k * rk_ref[...], ones_h)
    yield
    d = y - mean
    var = _group_sum(d * d, ones_h) * (1.0 / HEAD)
    yield
    yn = d * lax.rsqrt(var + RW_LN_EPS) * lng_ref[...] + lnb_ref[...]
    y_ref[...] = (yn + bonus_s * v) * g


def _lru_steps(slab, prm, y_ref, tail_ref, h_ref, unperm_ref, tb):
    cw_ref, cb_ref, wa_ref, ba_ref, wx_ref, bx_ref, lam_ref, ng_ref = prm
    nslab = LRU_WIDTH // LANES
    seg = tb // SUBLANES
    xb = jnp.concatenate([slab(c) for c in range(nslab)], axis=1)
    gate = jnp.concatenate([slab(nslab + c) for c in range(nslab)], axis=1)
    sub0 = _iota((SUBLANES, LRU_WIDTH), 0) == 0

    def group(x, g):
        return x[g * SUBLANES:(g + 1) * SUBLANES]

    wrapped = []
    for g in range(seg - (LRU_CONV - 1), seg):
        prev = tail_ref[(g - seg + LRU_CONV - 1) * SUBLANES:(g - seg + LRU_CONV) * SUBLANES, :]
        wrapped.append(jnp.where(sub0, pltpu.roll(prev, 1, 0), pltpu.roll(group(xb, g), 1, 0)))
    tail_ref[...] = xb[(seg - (LRU_CONV - 1)) * SUBLANES:, :]
    xc = cb_ref[...] + xb * cw_ref[LRU_CONV - 1:LRU_CONV, :]
    for j in range(1, LRU_CONV):
        shifted = jnp.concatenate(wrapped[LRU_CONV - 1 - j:] + [xb[0:(seg - j) * SUBLANES]], axis=0)
        xc = xc + shifted * cw_ref[LRU_CONV - 1 - j:LRU_CONV - j, :]

    half = LRU_WIDTH // 2
    yield LRU_WAITS[0]
    z_r = jnp.concatenate([_dot(xc[:, 0:half], wa_ref[0]), _dot(xc[:, half:], wa_ref[1])], axis=1)
    z_i = jnp.concatenate([_dot(xc[:, 0:half], wx_ref[0]), _dot(xc[:, half:], wx_ref[1])], axis=1)
    yield
    gate_r = _sigmoid(z_r + ba_ref[...])
    gate_i = _sigmoid(z_i + bx_ref[...])
    lam = lam_ref[...]
    softplus_neg_lam = jnp.maximum(-lam, 0.0) + jnp.log1p(jnp.exp(-jnp.abs(lam)))
    log_a = (-LRU_C * gate_r) * softplus_neg_lam
    a = jnp.exp(log_a)
    u = _sqrt_nonneg(1.0 - a * a) * (gate_i * xc)

    h0 = [group(u, 0)]
    ap = [group(a, 0)]
    for g in range(1, seg):
        h0.append(group(a, g) * h0[-1] + group(u, g))
        ap.append(group(a, g) * ap[-1])
    h_end, a_end = h0[-1], ap[-1]
    row = _iota((SUBLANES, LRU_WIDTH), 0)
    for d in (1, 2, 4):
        keep = row >= d
        h_end = a_end * jnp.where(keep, pltpu.roll(h_end, d, 0), 0.0) + h_end
        a_end = a_end * jnp.where(keep, pltpu.roll(a_end, d, 0), 1.0)
    h_true = h_end + a_end * h_ref[...]
    h_in = jnp.where(sub0, h_ref[...], pltpu.roll(h_true, 1, 0))
    h_ref[...] = h_true[SUBLANES - 1:SUBLANES, :]
    h = jnp.concatenate([h0[g] + ap[g] * h_in for g in range(seg)], axis=0)

    c2 = 2.0 * math.sqrt(2.0 / math.pi)
    y = h * (gate * _sigmoid(gate * (c2 + (c2 * 0.044715) * (gate * gate))))
    ones_h = _group_ones(half, HEAD)
    ysq = y * y
    yield LRU_WAITS[1]
    ms = jnp.concatenate([_group_sum(ysq[:, 0:half], ones_h),
                          _group_sum(ysq[:, half:], ones_h)], axis=1) * (1.0 / HEAD)
    yield
    y = y * lax.rsqrt(ms + NORM_EPS) * ng_ref[...]
    for c in range(nslab):
        unperm_ref[c] = y[:, c * LANES:(c + 1) * LANES]
    for s in range(SUBLANES):
        y_ref[s * seg:(s + 1) * seg, :] = jnp.concatenate(
            [unperm_ref[c, pl.ds(s, seg, stride=SUBLANES), :] for c in range(nslab)], axis=1)


def _gla_steps(p_ref, prm, y_ref, s_ref, tb):
    gkup_ref, gkb_ref, ng_ref = prm
    kw = GLA_KEY_WIDTH
    q = p_ref[:, 0:kw] * (GLA_DK ** -0.5)
    k = p_ref[:, kw:2 * kw]
    v = p_ref[:, 2 * kw:2 * kw + GLA_WIDTH]
    x = _dot3(p_ref[:, 768:GLA_PAD], gkup_ref[...]) + gkb_ref[...]
    yield
    la = (jnp.minimum(x, 0.0) - jnp.log1p(jnp.exp(-jnp.abs(x)))) * (1.0 / GLA_GATE_NORM)

    bc, bl = _chunk_cumsums(_chunk_tril_ones(tb), la)
    yield
    qi = q * jnp.exp(bc)
    ki = k * jnp.exp(-bc)
    kd = k * jnp.exp(bl - bc)
    dec = jnp.exp(bl)

    nh = GLA_WIDTH // HEAD
    lane_k = _iota((tb, kw), 1) // GLA_DK
    lane_v = _iota((tb, GLA_WIDTH), 1) // HEAD
    qs = jnp.concatenate([jnp.where(lane_k == h, qi, 0.0) for h in range(nh)], axis=0)
    st = _iota((nh * tb, tb), 0) % tb
    sj = _iota((nh * tb, tb), 1)
    causal = ((st // CHUNK) == (sj // CHUNK)) & (st >= sj)
    sc = _dot_nt(qs, ki)
    diag = (_iota((GLA_WIDTH, kw), 0) // HEAD) == (_iota((GLA_WIDTH, kw), 1) // GLA_DK)
    nc = tb // CHUNK
    upd = [jnp.where(diag, _dot_tn(v[c * CHUNK:(c + 1) * CHUNK], kd[c * CHUNK:(c + 1) * CHUNK]), 0.0)
           for c in range(nc)]
    yield
    o_full = _dot(jnp.where(causal, sc, 0.0), v)
    yield
    o_intra = jnp.where(lane_v == 0, o_full[0:tb], 0.0)
    for h in range(1, nh):
        o_intra = o_intra + jnp.where(lane_v == h, o_full[h * tb:(h + 1) * tb], 0.0)

    state = s_ref[...]
    outs = []
    for c in range(nc):
        rows = slice(c * CHUNK, (c + 1) * CHUNK)
        outs.append(o_intra[rows] + _dot_nt(qi[rows], state))
        state = state * dec[c * CHUNK:c * CHUNK + 1, :] + upd[c]
    s_ref[...] = state

    o = jnp.concatenate(outs, axis=0)
    ms = _group_sum(o * o, _group_ones(GLA_WIDTH, HEAD)) * (1.0 / HEAD)
    yield
    gate = p_ref[:, 512:768]
    y_ref[...] = o * lax.rsqrt(ms + NORM_EPS) * ng_ref[...] * (gate * _sigmoid(gate))


MIXER_STAGGER = (3, 3)
LRU_WAITS = (2, 6)
N_RW_PARAMS = 11
N_LRU_PARAMS = 8
N_GLA_PARAMS = 3


def _mixers_kernel(*refs, tb, nseq, stagger):
    prw_ref, plru_ref, pgla_ref = refs[0:3]
    o = 3
    rw_prm = refs[o:o + N_RW_PARAMS]
    o += N_RW_PARAMS
    lru_prm = refs[o:o + N_LRU_PARAMS]
    o += N_LRU_PARAMS
    gla_prm = refs[o:o + N_GLA_PARAMS]
    o += N_GLA_PARAMS
    yrw_ref, ylru_ref, ygla_ref = refs[o:o + 3]
    rw_tail, rw_state, lru_tail, lru_h, lru_unperm, gla_state = refs[o + 3:]

    @pl.when(pl.program_id(1) == 0)
    def _():
        for ref in (rw_tail, rw_state, lru_tail, lru_h, gla_state):
            ref[...] = jnp.zeros_like(ref)

    pending = [(0, [_rwkv_steps(prw_ref.at[q], rw_prm, yrw_ref.at[q], rw_tail.at[q], rw_state.at[q], tb)
                    for q in range(nseq)]),
               (stagger[0], [_gla_steps(pgla_ref.at[q], gla_prm, ygla_ref.at[q], gla_state.at[q], tb)
                             for q in range(nseq)]),
               (stagger[1], [_lru_steps(lambda c, q=q: plru_ref[c, q], lru_prm, ylru_ref.at[q],
                                        lru_tail.at[q], lru_h.at[q], lru_unperm.at[q], tb)
                             for q in range(nseq)])]
    pending.sort(key=lambda e: e[0])
    gens = []
    rnd = 0
    while gens or pending:
        while pending and pending[0][0] <= rnd:
            gens.extend([g, 0] for g in pending.pop(0)[1])
        alive = []
        for entry in gens:
            if entry[1] > 0:
                entry[1] -= 1
                alive.append(entry)
                continue
            try:
                wait = next(entry[0])
                entry[1] = (wait or 1) - 1
                alive.append(entry)
            except StopIteration:
                pass
        gens = alive
        rnd += 1


def _mixers(p_rw, p_lru, p_gla, params, layer, bsz, seq, tb, nseq):
    nt = seq // tb
    blk = lambda c: pl.BlockSpec((nseq, tb, c), lambda b, t: (b, t, 0))
    nslab = p_lru.shape[0]
    p_rw = p_rw.reshape(bsz, seq, p_rw.shape[-1])
    p_gla = p_gla.reshape(bsz, seq, p_gla.shape[-1])
    p_lru = p_lru.reshape(nslab, bsz, seq, LANES)
    widths = (RW_WIDTH, LRU_WIDTH, GLA_WIDTH)
    outs = pl.pallas_call(
        functools.partial(_mixers_kernel, tb=tb, nseq=nseq, stagger=MIXER_STAGGER),
        grid=(bsz // nseq, nt),
        in_specs=[blk(p_rw.shape[-1]),
                  pl.BlockSpec((nslab, nseq, tb, LANES), lambda b, t: (0, b, t, 0)),
                  blk(p_gla.shape[-1])] + [_layer_spec(q, layer) for q in params],
        out_specs=[blk(w) for w in widths],
        out_shape=[jax.ShapeDtypeStruct((bsz, seq, w), F32) for w in widths],
        scratch_shapes=[pltpu.VMEM((nseq, SUBLANES, p_rw.shape[-1]), F32),
                        pltpu.VMEM((nseq, 2, LANES, LANES), F32),
                        pltpu.VMEM((nseq, (LRU_CONV - 1) * SUBLANES, LRU_WIDTH), F32),
                        pltpu.VMEM((nseq, 1, LRU_WIDTH), F32),
                        pltpu.VMEM((nseq, LRU_WIDTH // LANES, tb, LANES), F32),
                        pltpu.VMEM((nseq, GLA_WIDTH, GLA_KEY_WIDTH), F32)],
        compiler_params=pltpu.CompilerParams(dimension_semantics=("arbitrary", "arbitrary"),
                                             vmem_limit_bytes=VMEM_LIMIT),
        name="mixers",
    )(p_rw, p_lru, p_gla, *params)
    return [y.reshape(bsz * seq, y.shape[-1]) for y in outs]


def _post_kernel(yrw_ref, ylru_ref, ygla_ref, x_ref, wout_ref, g2_ref, wg_ref, wu_ref, wd_ref,
                 gf_ref, o_ref, *, ff_chunk, final):
    y = jnp.concatenate([yrw_ref[...], ylru_ref[...], ygla_ref[...]], axis=1).astype(BF16)
    x = x_ref[...] + jnp.dot(y, wout_ref[...], preferred_element_type=F32)
    hn = _rms_rows(x, g2_ref[...]).astype(BF16)
    d_ff = wg_ref.shape[1]
    acc = x
    for c in range(d_ff // ff_chunk):
        cols = slice(c * ff_chunk, (c + 1) * ff_chunk)
        gate = jnp.dot(hn, wg_ref[:, cols], preferred_element_type=F32)
        up = jnp.dot(hn, wu_ref[:, cols], preferred_element_type=F32)
        h = (gate * _sigmoid(gate) * up).astype(BF16)
        acc = acc + jnp.dot(h, wd_ref[cols, :], preferred_element_type=F32)
    if final:
        acc = _rms_rows(acc, gf_ref[...])
    o_ref[...] = acc


def _post(y_rw, y_lru, y_gla, xf, w_out, g2, w_gate, w_up, w_down, gf, layer, tm, final):
    n, d = xf.shape
    row = lambda c: pl.BlockSpec((tm, c), lambda i: (i, 0))
    stacked = (w_out, g2, w_gate, w_up, w_down)
    return pl.pallas_call(
        functools.partial(_post_kernel, ff_chunk=256, final=final),
        grid=(n // tm,),
        in_specs=[row(y_rw.shape[1]), row(y_lru.shape[1]), row(y_gla.shape[1]), row(d)]
        + [_layer_spec(q, layer, single_buffer=q.ndim == 3 and q.shape[1] > 1) for q in stacked]
        + [_const_spec(gf.shape)],
        out_specs=row(d),
        out_shape=jax.ShapeDtypeStruct((n, d), F32),
        compiler_params=pltpu.CompilerParams(dimension_semantics=("arbitrary",),
                                             vmem_limit_bytes=VMEM_LIMIT),
        name="post",
    )(y_rw, y_lru, y_gla, xf, *stacked, gf)


def _rows(v):
    return v.reshape(v.shape[0], 1, -1).astype(F32)


def _block_diag_groups(w, per_group):
    nl, nb, c, _ = w.shape
    groups = nb // per_group
    w = w.reshape(nl, groups, per_group, c, c)
    eye = jnp.eye(per_group, dtype=w.dtype)
    out = jnp.einsum("lgbij,bk->lgbikj", w, eye)
    return out.reshape(nl, groups, per_group * c, per_group * c)


def kernel(x, norm1_g, w_in, rw_mu, rw_w0, rw_w_up, rw_a0, rw_a_up, rw_g_up, rw_k_k, rw_k_a, rw_r_k, rw_ln_g, rw_ln_b, lru_conv_w, lru_conv_b, lru_wa, lru_ba, lru_wx, lru_bx, lru_lam, lru_norm_g, gla_gk_up, gla_gk_b, gla_norm_g, w_out, norm2_g, ffn_w_gate, ffn_w_up, ffn_w_down, final_norm_g):
    bsz, seq, d = x.shape
    depth = w_in.shape[0]
    n = bsz * seq
    tb = 256
    tm = 1024 if n % 1024 == 0 else tb
    tm_in = tm
    nseq = 2 if bsz % 2 == 0 else 1

    pad0 = ((0, 0),)
    mixer_params = (
        _rows(rw_mu), _rows(rw_w0),
        jnp.pad(rw_w_up, pad0 + ((0, HEAD), (0, 0))).astype(BF16),
        _rows(rw_a0),
        jnp.pad(rw_a_up, pad0 + ((HEAD, 0), (0, 0))).astype(BF16),
        rw_g_up.astype(BF16), _rows(rw_k_k), _rows(rw_k_a), _rows(rw_r_k), _rows(rw_ln_g), _rows(rw_ln_b),
        lru_conv_w.astype(F32), _rows(lru_conv_b),
        _block_diag_groups(lru_wa, 4).astype(BF16), _rows(lru_ba),
        _block_diag_groups(lru_wx, 4).astype(BF16), _rows(lru_bx), _rows(lru_lam), _rows(lru_norm_g),
        jnp.pad(gla_gk_up, pad0 + ((0, LANES - GLA_GATE_RANK), (0, 0))).astype(F32), _rows(gla_gk_b),
        _rows(jnp.tile(gla_norm_g, (1, GLA_WIDTH // HEAD))))
    g1, g2 = _rows(norm1_g), _rows(norm2_g)
    w_out_b, w_gate_b, w_up_b, w_down_b = (w.astype(BF16) for w in (w_out, ffn_w_gate, ffn_w_up, ffn_w_down))
    gf = final_norm_g.reshape(1, -1).astype(F32)

    xf = x.reshape(n, d)
    for l in range(depth):
        p_rw, p_lru, p_gla = _inproj(xf, g1, w_in.astype(F32), l, tm_in, tb)
        y_rw, y_lru, y_gla = _mixers(p_rw, p_lru, p_gla, mixer_params, l, bsz, seq, tb, nseq)
        xf = _post(y_rw, y_lru, y_gla, xf, w_out_b, g2, w_gate_b, w_up_b, w_down_b, gf, l, tm,
                   final=(l == depth - 1))
    return xf.reshape(bsz, seq, d)
```

```python
import functools
import math

import jax
import jax.numpy as jnp
from jax import lax
from jax.experimental import pallas as pl
from jax.experimental.pallas import tpu as pltpu

F32 = jnp.float32
BF16 = jnp.bfloat16

HEAD = 64
NORM_EPS = 1e-6
RW_WIDTH = 256
RW_DECAY_SCALE = math.exp(-0.5)
RW_LN_EPS = 64e-5
LRU_WIDTH = 512
LRU_CONV = 4
LRU_C = 8.0
GLA_WIDTH = 256
GLA_KEY_WIDTH = 128
GLA_DK = 32
GLA_GATE_RANK = 16
GLA_GATE_NORM = 16.0
GLA_PAD = 896

CHUNK = 64
LANES = 128
SUBLANES = 8
VMEM_LIMIT = 56 * 1024 * 1024


def _dot(a, b):
    return jnp.dot(a.astype(BF16), b.astype(BF16), preferred_element_type=F32)


def _dot_nt(a, b):
    return lax.dot_general(a.astype(BF16), b.astype(BF16), (((1,), (1,)), ((), ())),
                           preferred_element_type=F32)


def _dot_tn(a, b):
    return lax.dot_general(a.astype(BF16), b.astype(BF16), (((0,), (0,)), ((), ())),
                           preferred_element_type=F32)


def _split(x):
    hi = x.astype(BF16)
    lo = (x - hi.astype(F32)).astype(BF16)
    return hi, lo


def _group_sum(x, ones):
    return jnp.dot(x.astype(BF16), ones, preferred_element_type=F32)


def _chunk_cumsums(tril_c, x):
    hi, lo = _split(x)
    cs = jnp.dot(tril_c, hi, preferred_element_type=F32) + jnp.dot(tril_c, lo, preferred_element_type=F32)
    nc = x.shape[0] // CHUNK
    cl = jnp.concatenate(
        [jnp.broadcast_to(cs[(c + 1) * CHUNK - 1:(c + 1) * CHUNK], (CHUNK, x.shape[1])) for c in range(nc)],
        axis=0)
    return cs, cl


def _dot3(a, b):
    ah, al = _split(a)
    bh, bl = _split(b)
    return (jnp.dot(ah, bh, preferred_element_type=F32)
            + jnp.dot(al, bh, preferred_element_type=F32)
            + jnp.dot(ah, bl, preferred_element_type=F32))


def _iota(shape, dim):
    return lax.broadcasted_iota(jnp.int32, shape, dim)


def _group_ones(n, group):
    r = _iota((n, n), 0) // group
    c = _iota((n, n), 1) // group
    return jnp.where(r == c, 1.0, 0.0).astype(BF16)


def _chunk_tril_ones(tb):
    tr = _iota((tb, tb), 0)
    tc = _iota((tb, tb), 1)
    return jnp.where(((tr // CHUNK) == (tc // CHUNK)) & (tr >= tc), 1.0, 0.0).astype(BF16)


def _rms_rows(x, g):
    ms = jnp.mean(x * x, axis=-1, keepdims=True)
    return x * lax.rsqrt(ms + NORM_EPS) * g


def _shift_rows(x, prev_tail, j):
    xs = pltpu.roll(x, j, 0)
    fix = pltpu.roll(prev_tail, j, 0)
    row = _iota((SUBLANES, x.shape[1]), 0)
    head = jnp.where(row < j, fix, xs[0:SUBLANES])
    return jnp.concatenate([head, xs[SUBLANES:]], axis=0)


def _sigmoid(x):
    return 1.0 / (1.0 + jnp.exp2(x * (-math.log2(math.e))))


def _sqrt_nonneg(y):
    return jnp.where(y > 0.0, y * lax.rsqrt(y), 0.0)


def _const_spec(shape):
    nd = len(shape)
    return pl.BlockSpec(shape, lambda *_: (0,) * nd)


def _layer_spec(arr, layer, single_buffer=False):
    nd = arr.ndim - 1
    mode = pl.Buffered(1) if single_buffer else None
    return pl.BlockSpec((None,) + arr.shape[1:], lambda *_: (layer,) + (0,) * nd, pipeline_mode=mode)


RW_PROJ_W = 1024


def _inproj_kernel(x_ref, g_ref, w_ref, prw_ref, plru_ref, pgla_ref, *, tm, tb):
    hn = _rms_rows(x_ref[...], g_ref[...]).astype(BF16)
    lru_w = plru_ref.shape[0] * LANES
    prw_ref[...] = jnp.dot(hn, w_ref[:, 0:RW_PROJ_W], preferred_element_type=F32)
    p_lru = jnp.dot(hn, w_ref[:, RW_PROJ_W:RW_PROJ_W + lru_w], preferred_element_type=F32)
    seg = tb // SUBLANES
    for blk in range(tm // tb):
        for s in range(SUBLANES):
            rows = slice(blk * tb + s * seg, blk * tb + (s + 1) * seg)
            for c in range(lru_w // LANES):
                plru_ref[c, pl.ds(blk * tb + s, seg, stride=SUBLANES), :] = (
                    p_lru[rows, c * LANES:(c + 1) * LANES])
    gla0 = RW_PROJ_W + lru_w
    qkv_w = 2 * GLA_KEY_WIDTH + GLA_WIDTH
    pgla_ref[:, 0:qkv_w] = jnp.dot(hn, w_ref[:, gla0:gla0 + qkv_w], preferred_element_type=F32)
    rest = jnp.dot(hn, w_ref[:, gla0 + qkv_w:], preferred_element_type=F32)
    pgla_ref[:, qkv_w:qkv_w + GLA_WIDTH] = rest[:, GLA_GATE_RANK:GLA_GATE_RANK + GLA_WIDTH]
    pgla_ref[:, qkv_w + GLA_WIDTH:] = rest[:, 0:LANES]


def _inproj(xf, g, w, layer, tm, tb):
    n, d = xf.shape
    lru_w = 2 * LRU_WIDTH
    gla_w = GLA_PAD
    row = lambda c: pl.BlockSpec((tm, c), lambda i: (i, 0))
    return pl.pallas_call(
        functools.partial(_inproj_kernel, tm=tm, tb=tb),
        grid=(n // tm,),
        in_specs=[row(d), _layer_spec(g, layer), _layer_spec(w, layer)],
        out_specs=[row(RW_PROJ_W), pl.BlockSpec((lru_w // LANES, tm, LANES), lambda i: (0, i, 0)), row(gla_w)],
        out_shape=[jax.ShapeDtypeStruct((n, RW_PROJ_W), F32),
                   jax.ShapeDtypeStruct((lru_w // LANES, n, LANES), F32),
                   jax.ShapeDtypeStruct((n, gla_w), F32)],
        compiler_params=pltpu.CompilerParams(dimension_semantics=("arbitrary",),
                                             vmem_limit_bytes=VMEM_LIMIT),
        name="inproj",
    )(xf, g, w)


def _rwkv_steps(p_ref, prm, y_ref, tail_ref, s_ref, tb):
    (mu_ref, w0_ref, wup_ref, a0_ref, aup_ref, gup_ref, kk_ref, ka_ref, rk_ref, lng_ref,
     lnb_ref) = prm
    p = p_ref[...]
    shifted = _shift_rows(p, tail_ref[...], 1)
    tail_ref[...] = p[tb - SUBLANES:tb, :]
    ps = p + (shifted - p) * mu_ref[...]
    r = ps[:, 0:256]
    k = ps[:, 256:512]
    v = ps[:, 512:768]
    lo_wa = ps[:, 768:896]
    g_lo = ps[:, 896:1024]
    ones_h = _group_ones(RW_WIDTH, HEAD)
    kk = k * kk_ref[...]
    w_raw = w0_ref[...] + _dot(jnp.tanh(lo_wa), wup_ref[...])
    a = _sigmoid(a0_ref[...] + _dot(lo_wa, aup_ref[...]))
    g = _dot(_sigmoid(g_lo), gup_ref[...])
    kk_sq = _group_sum(kk * kk, ones_h)
    yield
    lw = -RW_DECAY_SCALE * _sigmoid(w_raw)
    kk = kk * lax.rsqrt(jnp.maximum(kk_sq, 1e-24))
    k = k * (1.0 + (a - 1.0) * ka_ref[...])
    a_neg = -kk
    b = kk * a

    cs, cl = _chunk_cumsums(_chunk_tril_ones(tb), lw)
    yield
    inv = jnp.exp(-cs)
    dec = jnp.exp(cl - cs)
    wl = jnp.exp(cl)
    at, rt, bt, kt = a_neg * jnp.exp(cs - lw), r * jnp.exp(cs), b * inv, k * inv
    bd, kd = b * dec, k * dec

    rr = _iota((LANES, LANES), 0)
    cc = _iota((LANES, LANES), 1)
    same = (rr // CHUNK) == (cc // CHUNK)
    strict = same & (cc < rr)
    incl = same & (cc <= rr)
    eye = jnp.where(rr == cc, 1.0, 0.0).astype(F32)
    m0 = _iota((CHUNK, LANES), 1) < HEAD
    zeros = jnp.zeros((LANES, LANES), F32)

    def stack(x):
        return jnp.concatenate([jnp.where(m0, x, 0.0), jnp.where(m0, 0.0, x)], axis=0)

    def tile(x):
        return jnp.concatenate([x, x], axis=0)

    nc = tb // CHUNK
    idx = [(c, pair) for c in range(nc) for pair in range(2)]

    def blk(x, c, pair):
        return x[c * CHUNK:(c + 1) * CHUNK, pair * LANES:(pair + 1) * LANES]

    a_st = [stack(blk(at, *i)) for i in idx]
    ar = [jnp.concatenate([a_st[j], stack(blk(rt, *i))], axis=0) for j, i in enumerate(idx)]
    gm = [_dot_nt(ar[j], jnp.concatenate([tile(blk(bt, *i)), tile(blk(kt, *i))], axis=0))
          for j, i in enumerate(idx)]
    yield
    n_mat = [jnp.where(strict, m[0:LANES, 0:LANES], 0.0) for m in gm]
    l_ak = [jnp.where(strict, m[0:LANES, LANES:], 0.0) for m in gm]
    l_rbk = [jnp.concatenate([jnp.where(incl, m[LANES:, 0:LANES], 0.0),
                              jnp.where(incl, m[LANES:, LANES:], 0.0)], axis=1) for m in gm]
    v_bd = [stack(blk(v, *i)) for i in idx]
    lv = [_dot(l, vb) for l, vb in zip(l_ak, v_bd)]

    d4 = (rr // 4) == (cc // 4)
    nd = [jnp.where(d4, m, 0.0) for m in n_mat]
    nd2 = [_dot(m, m) for m in nd]
    yield
    t_inv = [eye + m + _dot(m2, eye + m) for m, m2 in zip(nd, nd2)]
    yield
    s = 4
    while s < CHUNK:
        off = ((rr // (2 * s)) == (cc // (2 * s))) & ((rr // s) != (cc // s))
        if s < SUBLANES:
            nt = [_dot(jnp.where(off, m, 0.0), t) for m, t in zip(n_mat, t_inv)]
            yield
            t_inv = [t + _dot(t, m) for t, m in zip(t_inv, nt)]
            yield
        else:
            def low(m):
                return jnp.concatenate([m[k + s:k + 2 * s] for k in range(0, LANES, 2 * s)], axis=0)

            il = _iota((LANES // 2, LANES), 0)
            rl = (il // s) * (2 * s) + s + il % s
            cl_ = _iota((LANES // 2, LANES), 1)
            off_low = ((rl // (2 * s)) == (cl_ // (2 * s))) & ((rl // s) != (cl_ // s))
            zs = jnp.zeros((s, LANES), F32)
            nt = [_dot(jnp.where(off_low, low(m), 0.0), t) for m, t in zip(n_mat, t_inv)]
            yield
            upd = []
            for t, m in zip(t_inv, nt):
                m_full = jnp.concatenate(
                    [piece for k in range(LANES // (2 * s)) for piece in (zs, m[k * s:(k + 1) * s])], axis=0)
                upd.append(_dot(low(t), m_full))
            yield
            t_inv = [jnp.concatenate(
                [piece for k in range(LANES // (2 * s))
                 for piece in (t[2 * k * s:2 * k * s + s], t[2 * k * s + s:2 * (k + 1) * s] + u[k * s:(k + 1) * s])],
                axis=0) for t, u in zip(t_inv, upd)]
        s *= 2

    x = [_dot(t, jnp.concatenate([a_, l], axis=1)) for t, a_, l in zip(t_inv, a_st, lv)]
    yield
    z = [jnp.concatenate([xx, jnp.concatenate([zeros, vb], axis=1)], axis=0) for xx, vb in zip(x, v_bd)]
    y2 = [_dot(l, zz) for l, zz in zip(l_rbk, z)]
    ms = [_dot_tn(zz, jnp.concatenate([stack(blk(bd, *i)), stack(blk(kd, *i))], axis=0))
          for zz, i in zip(z, idx)]
    yield

    outs = []
    states = [s_ref[0], s_ref[1]]
    for c in range(nc):
        halves = []
        for pair in range(2):
            j = 2 * c + pair
            state = states[pair]
            y_bd = _dot_nt(ar[j][LANES:] + y2[j][:, 0:LANES], state) + y2[j][:, LANES:]
            halves.append(y_bd[0:CHUNK] + y_bd[CHUNK:])
            states[pair] = (state * blk(wl, c, pair)[0:1] + _dot(state, ms[j][0:LANES])
                            + ms[j][LANES:])
        outs.append(jnp.concatenate(halves, axis=1))
        yield
    s_ref[0] = states[0]
    s_ref[1] = states[1]

    y = jnp.concatenate(outs, axis=0)
    mean = _group_sum(y, ones_h) * (1.0 / HEAD)
    bonus_s = _group_sum(r * k * rk_ref[...], ones_h)
    yield
    d = y - mean
    var = _group_sum(d * d, ones_h) * (1.0 / HEAD)
    yield
    yn = d * lax.rsqrt(var + RW_LN_EPS) * lng_ref[...] + lnb_ref[...]
    y_ref[...] = (yn + bonus_s * v) * g


def _lru_steps(slab, prm, y_ref, tail_ref, h_ref, unperm_ref, tb):
    cw_ref, cb_ref, wa_ref, ba_ref, wx_ref, bx_ref, lam_ref, ng_ref = prm
    nslab = LRU_WIDTH // LANES
    seg = tb // SUBLANES
    xb = jnp.concatenate([slab(c) for c in range(nslab)], axis=1)
    gate = jnp.concatenate([slab(nslab + c) for c in range(nslab)], axis=1)
    sub0 = _iota((SUBLANES, LRU_WIDTH), 0) == 0

    def group(x, g):
        return x[g * SUBLANES:(g + 1) * SUBLANES]

    wrapped = []
    for g in range(seg - (LRU_CONV - 1), seg):
        prev = tail_ref[(g - seg + LRU_CONV - 1) * SUBLANES:(g - seg + LRU_CONV) * SUBLANES, :]
        wrapped.append(jnp.where(sub0, pltpu.roll(prev, 1, 0), pltpu.roll(group(xb, g), 1, 0)))
    tail_ref[...] = xb[(seg - (LRU_CONV - 1)) * SUBLANES:, :]
    xc = cb_ref[...] + xb * cw_ref[LRU_CONV - 1:LRU_CONV, :]
    for j in range(1, LRU_CONV):
        shifted = jnp.concatenate(wrapped[LRU_CONV - 1 - j:] + [xb[0:(seg - j) * SUBLANES]], axis=0)
        xc = xc + shifted * cw_ref[LRU_CONV - 1 - j:LRU_CONV - j, :]

    half = LRU_WIDTH // 2
    yield LRU_WAITS[0]
    z_r = jnp.concatenate([_dot(xc[:, 0:half], wa_ref[0]), _dot(xc[:, half:], wa_ref[1])], axis=1)
    z_i = jnp.concatenate([_dot(xc[:, 0:half], wx_ref[0]), _dot(xc[:, half:], wx_ref[1])], axis=1)
    yield
    gate_r = _sigmoid(z_r + ba_ref[...])
    gate_i = _sigmoid(z_i + bx_ref[...])
    lam = lam_ref[...]
    softplus_neg_lam = jnp.maximum(-lam, 0.0) + jnp.log1p(jnp.exp(-jnp.abs(lam)))
    log_a = (-LRU_C * gate_r) * softplus_neg_lam
    a = jnp.exp(log_a)
    u = _sqrt_nonneg(1.0 - a * a) * (gate_i * xc)

    h0 = [group(u, 0)]
    ap = [group(a, 0)]
    for g in range(1, seg):
        h0.append(group(a, g) * h0[-1] + group(u, g))
        ap.append(group(a, g) * ap[-1])
    h_end, a_end = h0[-1], ap[-1]
    row = _iota((SUBLANES, LRU_WIDTH), 0)
    for d in (1, 2, 4):
        keep = row >= d
        h_end = a_end * jnp.where(keep, pltpu.roll(h_end, d, 0), 0.0) + h_end
        a_end = a_end * jnp.where(keep, pltpu.roll(a_end, d, 0), 1.0)
    h_true = h_end + a_end * h_ref[...]
    h_in = jnp.where(sub0, h_ref[...], pltpu.roll(h_true, 1, 0))
    h_ref[...] = h_true[SUBLANES - 1:SUBLANES, :]
    h = jnp.concatenate([h0[g] + ap[g] * h_in for g in range(seg)], axis=0)

    c2 = 2.0 * math.sqrt(2.0 / math.pi)
    y = h * (gate * _sigmoid(gate * (c2 + (c2 * 0.044715) * (gate * gate))))
    ones_h = _group_ones(half, HEAD)
    ysq = y * y
    yield LRU_WAITS[1]
    ms = jnp.concatenate([_group_sum(ysq[:, 0:half], ones_h),
                          _group_sum(ysq[:, half:], ones_h)], axis=1) * (1.0 / HEAD)
    yield
    y = y * lax.rsqrt(ms + NORM_EPS) * ng_ref[...]
    for c in range(nslab):
        unperm_ref[c] = y[:, c * LANES:(c + 1) * LANES]
    for s in range(SUBLANES):
        y_ref[s * seg:(s + 1) * seg, :] = jnp.concatenate(
            [unperm_ref[c, pl.ds(s, seg, stride=SUBLANES), :] for c in range(nslab)], axis=1)


def _gla_steps(p_ref, prm, y_ref, s_ref, tb):
    gkup_ref, gkb_ref, ng_ref = prm
    kw = GLA_KEY_WIDTH
    q = p_ref[:, 0:kw] * (GLA_DK ** -0.5)
    k = p_ref[:, kw:2 * kw]
    v = p_ref[:, 2 * kw:2 * kw + GLA_WIDTH]
    x = _dot3(p_ref[:, 768:GLA_PAD], gkup_ref[...]) + gkb_ref[...]
    yield
    la = (jnp.minimum(x, 0.0) - jnp.log1p(jnp.exp(-jnp.abs(x)))) * (1.0 / GLA_GATE_NORM)

    bc, bl = _chunk_cumsums(_chunk_tril_ones(tb), la)
    yield
    qi = q * jnp.exp(bc)
    ki = k * jnp.exp(-bc)
    kd = k * jnp.exp(bl - bc)
    dec = jnp.exp(bl)

    nh = GLA_WIDTH // HEAD
    lane_k = _iota((tb, kw), 1) // GLA_DK
    lane_v = _iota((tb, GLA_WIDTH), 1) // HEAD
    qs = jnp.concatenate([jnp.where(lane_k == h, qi, 0.0) for h in range(nh)], axis=0)
    st = _iota((nh * tb, tb), 0) % tb
    sj = _iota((nh * tb, tb), 1)
    causal = ((st // CHUNK) == (sj // CHUNK)) & (st >= sj)
    sc = _dot_nt(qs, ki)
    diag = (_iota((GLA_WIDTH, kw), 0) // HEAD) == (_iota((GLA_WIDTH, kw), 1) // GLA_DK)
    nc = tb // CHUNK
    upd = [jnp.where(diag, _dot_tn(v[c * CHUNK:(c + 1) * CHUNK], kd[c * CHUNK:(c + 1) * CHUNK]), 0.0)
           for c in range(nc)]
    yield
    o_full = _dot(jnp.where(causal, sc, 0.0), v)
    yield
    o_intra = jnp.where(lane_v == 0, o_full[0:tb], 0.0)
    for h in range(1, nh):
        o_intra = o_intra + jnp.where(lane_v == h, o_full[h * tb:(h + 1) * tb], 0.0)

    state = s_ref[...]
    outs = []
    for c in range(nc):
        rows = slice(c * CHUNK, (c + 1) * CHUNK)
        outs.append(o_intra[rows] + _dot_nt(qi[rows], state))
        state = state * dec[c * CHUNK:c * CHUNK + 1, :] + upd[c]
    s_ref[...] = state

    o = jnp.concatenate(outs, axis=0)
    ms = _group_sum(o * o, _group_ones(GLA_WIDTH, HEAD)) * (1.0 / HEAD)
    yield
    gate = p_ref[:, 512:768]
    y_ref[...] = o * lax.rsqrt(ms + NORM_EPS) * ng_ref[...] * (gate * _sigmoid(gate))


MIXER_STAGGER = (3, 3)
LRU_WAITS = (2, 6)
N_RW_PARAMS = 11
N_LRU_PARAMS = 8
N_GLA_PARAMS = 3


def _mixers_kernel(*refs, tb, nseq, stagger):
    prw_ref, plru_ref, pgla_ref = refs[0:3]
    o = 3
    rw_prm = refs[o:o + N_RW_PARAMS]
    o += N_RW_PARAMS
    lru_prm = refs[o:o + N_LRU_PARAMS]
    o += N_LRU_PARAMS
    gla_prm = refs[o:o + N_GLA_PARAMS]
    o += N_GLA_PARAMS
    yrw_ref, ylru_ref, ygla_ref = refs[o:o + 3]
    rw_tail, rw_state, lru_tail, lru_h, lru_unperm, gla_state = refs[o + 3:]

    @pl.when(pl.program_id(1) == 0)
    def _():
        for ref in (rw_tail, rw_state, lru_tail, lru_h, gla_state):
            ref[...] = jnp.zeros_like(ref)

    pending = [(0, [_rwkv_steps(prw_ref.at[q], rw_prm, yrw_ref.at[q], rw_tail.at[q], rw_state.at[q], tb)
                    for q in range(nseq)]),
               (stagger[0], [_gla_steps(pgla_ref.at[q], gla_prm, ygla_ref.at[q], gla_state.at[q], tb)
                             for q in range(nseq)]),
               (stagger[1], [_lru_steps(lambda c, q=q: plru_ref[c, q], lru_prm, ylru_ref.at[q],
                                        lru_tail.at[q], lru_h.at[q], lru_unperm.at[q], tb)
                             for q in range(nseq)])]
    pending.sort(key=lambda e: e[0])
    gens = []
    rnd = 0
    while gens or pending:
        while pending and pending[0][0] <= rnd:
            gens.extend([g, 0] for g in pending.pop(0)[1])
        alive = []
        for entry in gens:
            if entry[1] > 0:
                entry[1] -= 1
                alive.append(entry)
                continue
            try:
                wait = next(entry[0])
                entry[1] = (wait or 1) - 1
                alive.append(entry)
            except StopIteration:
                pass
        gens = alive
        rnd += 1


def _mixers(p_rw, p_lru, p_gla, params, layer, bsz, seq, tb, nseq):
    nt = seq // tb
    blk = lambda c: pl.BlockSpec((nseq, tb, c), lambda b, t: (b, t, 0))
    nslab = p_lru.shape[0]
    p_rw = p_rw.reshape(bsz, seq, p_rw.shape[-1])
    p_gla = p_gla.reshape(bsz, seq, p_gla.shape[-1])
    p_lru = p_lru.reshape(nslab, bsz, seq, LANES)
    widths = (RW_WIDTH, LRU_WIDTH, GLA_WIDTH)
    outs = pl.pallas_call(
        functools.partial(_mixers_kernel, tb=tb, nseq=nseq, stagger=MIXER_STAGGER),
        grid=(bsz // nseq, nt),
        in_specs=[blk(p_rw.shape[-1]),
                  pl.BlockSpec((nslab, nseq, tb, LANES), lambda b, t: (0, b, t, 0)),
                  blk(p_gla.shape[-1])] + [_layer_spec(q, layer) for q in params],
        out_specs=[blk(w) for w in widths],
        out_shape=[jax.ShapeDtypeStruct((bsz, seq, w), F32) for w in widths],
        scratch_shapes=[pltpu.VMEM((nseq, SUBLANES, p_rw.shape[-1]), F32),
                        pltpu.VMEM((nseq, 2, LANES, LANES), F32),
                        pltpu.VMEM((nseq, (LRU_CONV - 1) * SUBLANES, LRU_WIDTH), F32),
                        pltpu.VMEM((nseq, 1, LRU_WIDTH), F32),
                        pltpu.VMEM((nseq, LRU_WIDTH // LANES, tb, LANES), F32),
                        pltpu.VMEM((nseq, GLA_WIDTH, GLA_KEY_WIDTH), F32)],
        compiler_params=pltpu.CompilerParams(dimension_semantics=("arbitrary", "arbitrary"),
                                             vmem_limit_bytes=VMEM_LIMIT),
        name="mixers",
    )(p_rw, p_lru, p_gla, *params)
    return [y.reshape(bsz * seq, y.shape[-1]) for y in outs]


def _post_kernel(yrw_ref, ylru_ref, ygla_ref, x_ref, wout_ref, g2_ref, wg_ref, wu_ref, wd_ref,
                 gf_ref, o_ref, *, ff_chunk, final):
    y = jnp.concatenate([yrw_ref[...], ylru_ref[...], ygla_ref[...]], axis=1).astype(BF16)
    x = x_ref[...] + jnp.dot(y, wout_ref[...], preferred_element_type=F32)
    hn = _rms_rows(x, g2_ref[...]).astype(BF16)
    d_ff = wg_ref.shape[1]
    acc = x
    for c in range(d_ff // ff_chunk):
        cols = slice(c * ff_chunk, (c + 1) * ff_chunk)
        gate = jnp.dot(hn, wg_ref[:, cols], preferred_element_type=F32)
        up = jnp.dot(hn, wu_ref[:, cols], preferred_element_type=F32)
        h = (gate * _sigmoid(gate) * up).astype(BF16)
        acc = acc + jnp.dot(h, wd_ref[cols, :], preferred_element_type=F32)
    if final:
        acc = _rms_rows(acc, gf_ref[...])
    o_ref[...] = acc


def _post(y_rw, y_lru, y_gla, xf, w_out, g2, w_gate, w_up, w_down, gf, layer, tm, final):
    n, d = xf.shape
    row = lambda c: pl.BlockSpec((tm, c), lambda i: (i, 0))
    stacked = (w_out, g2, w_gate, w_up, w_down)
    return pl.pallas_call(
        functools.partial(_post_kernel, ff_chunk=256, final=final),
        grid=(n // tm,),
        in_specs=[row(y_rw.shape[1]), row(y_lru.shape[1]), row(y_gla.shape[1]), row(d)]
        + [_layer_spec(q, layer, single_buffer=q.ndim == 3 and q.shape[1] > 1) for q in stacked]
        + [_const_spec(gf.shape)],
        out_specs=row(d),
        out_shape=jax.ShapeDtypeStruct((n, d), F32),
        compiler_params=pltpu.CompilerParams(dimension_semantics=("arbitrary",),
                                             vmem_limit_bytes=VMEM_LIMIT),
        name="post",
    )(y_rw, y_lru, y_gla, xf, *stacked, gf)


def _rows(v):
    return v.reshape(v.shape[0], 1, -1).astype(F32)


def _block_diag_groups(w, per_group):
    nl, nb, c, _ = w.shape
    groups = nb // per_group
    w = w.reshape(nl, groups, per_group, c, c)
    eye = jnp.eye(per_group, dtype=w.dtype)
    out = jnp.einsum("lgbij,bk->lgbikj", w, eye)
    return out.reshape(nl, groups, per_group * c, per_group * c)


def kernel(x, norm1_g, w_in, rw_mu, rw_w0, rw_w_up, rw_a0, rw_a_up, rw_g_up, rw_k_k, rw_k_a, rw_r_k, rw_ln_g, rw_ln_b, lru_conv_w, lru_conv_b, lru_wa, lru_ba, lru_wx, lru_bx, lru_lam, lru_norm_g, gla_gk_up, gla_gk_b, gla_norm_g, w_out, norm2_g, ffn_w_gate, ffn_w_up, ffn_w_down, final_norm_g):
    bsz, seq, d = x.shape
    depth = w_in.shape[0]
    n = bsz * seq
    tb = 256
    tm = 1024 if n % 1024 == 0 else tb
    tm_in = tm
    nseq = 2 if bsz % 2 == 0 else 1

    pad0 = ((0, 0),)
    w_in_b = jnp.pad(w_in, pad0 + pad0 + ((0, -w_in.shape[2] % LANES),)).astype(BF16)
    mixer_params = (
        _rows(rw_mu), _rows(rw_w0),
        jnp.pad(rw_w_up, pad0 + ((0, HEAD), (0, 0))).astype(BF16),
        _rows(rw_a0),
        jnp.pad(rw_a_up, pad0 + ((HEAD, 0), (0, 0))).astype(BF16),
        rw_g_up.astype(BF16), _rows(rw_k_k), _rows(rw_k_a), _rows(rw_r_k), _rows(rw_ln_g), _rows(rw_ln_b),
        lru_conv_w.astype(F32), _rows(lru_conv_b),
        _block_diag_groups(lru_wa, 4).astype(BF16), _rows(lru_ba),
        _block_diag_groups(lru_wx, 4).astype(BF16), _rows(lru_bx), _rows(lru_lam), _rows(lru_norm_g),
        jnp.pad(gla_gk_up, pad0 + ((0, LANES - GLA_GATE_RANK), (0, 0))).astype(F32), _rows(gla_gk_b),
        _rows(jnp.tile(gla_norm_g, (1, GLA_WIDTH // HEAD))))
    g1, g2 = _rows(norm1_g), _rows(norm2_g)
    w_out_b, w_gate_b, w_up_b, w_down_b = (w.astype(BF16) for w in (w_out, ffn_w_gate, ffn_w_up, ffn_w_down))
    gf = final_norm_g.reshape(1, -1).astype(F32)

    xf = x.reshape(n, d)
    for l in range(depth):
        p_rw, p_lru, p_gla = _inproj(xf, g1, w_in_b, l, tm_in, tb)
        y_rw, y_lru, y_gla = _mixers(p_rw, p_lru, p_gla, mixer_params, l, bsz, seq, tb, nseq)
        xf = _post(y_rw, y_lru, y_gla, xf, w_out_b, g2, w_gate_b, w_up_b, w_down_b, gf, l, tm,
                   final=(l == depth - 1))
    return xf.reshape(bsz, seq, d)
```
